```python
import math
import jax, jax.numpy as jnp
from jax import lax
import numpy as np

D_MODEL = 1024
BATCH = 8
SEQ = 2048
DEPTH = 4

GRID_W = 64
CTX_LEN = 256
N_MIXERS = 3
N_LAYERS_A = (DEPTH + 2) // 3
N_LAYERS_B = (DEPTH + 1) // 3
N_LAYERS_C = DEPTH // 3
CHUNK = 64
EPS = 1e-6
CONV_W = 3

A_HEADS = 8
A_QK = D_MODEL // 2
A_V = D_MODEL
A_DK = A_QK // A_HEADS
A_DV = A_V // A_HEADS
A_IN = 2 * A_QK + 2 * A_V + 4 * A_HEADS

B_HEADS = 4
B_QK = D_MODEL // 2
B_V = D_MODEL
B_DK = B_QK // B_HEADS
B_DV = B_V // B_HEADS
B_RANK = 16
B_TAU = 16.0
B_IN = 2 * B_QK + 2 * B_V + 2 * B_RANK

C_FF = 64
C_BANDS = 16
C_EMB = 1 + 2 * C_BANDS
C_INNER = 2
C_TARGET = 1e-2
C_FAST = 0.3
C_SLOW = 1.5

N_EXPERTS = 16
N_GROUPS = 4
EXPERTS_PER_GROUP = N_EXPERTS // N_GROUPS
TOP_K = 2
EXPERT_FF = 1024
MOE_BLOCK = 512

kernel_name = "hybrid_mlstm_gla_hyena_moe_dit"

F32 = jnp.float32


def rmsnorm(x, g):
    xf = x.astype(F32)
    y = xf * lax.rsqrt(jnp.mean(jnp.square(xf), -1, keepdims=True) + EPS)
    return y.astype(x.dtype) * g


def short_conv(x, w, b):
    xp = jnp.pad(x, ((0, 0), (1, 1), (0, 0)))
    return xp[:, :-2] * w[0] + xp[:, 1:-1] * w[1] + xp[:, 2:] * w[2] + b


def to_heads(t, n_heads):
    bsz, s, w = t.shape
    return t.reshape(bsz, s, n_heads, w // n_heads).transpose(0, 2, 1, 3)


def head_rmsnorm(y, g):
    y = y * lax.rsqrt(jnp.mean(jnp.square(y), -1, keepdims=True) + EPS)
    bsz, h, s, d = y.shape
    return y.transpose(0, 2, 1, 3).reshape(bsz, s, h * d).astype(g.dtype) * g


def raster_to_columns(t):
    bsz, s, w = t.shape
    rows = s // GRID_W
    return t.reshape(bsz, rows, GRID_W, w).transpose(0, 2, 1, 3).reshape(bsz, s, w)


def columns_to_raster(t):
    bsz, s, w = t.shape
    rows = s // GRID_W
    return t.reshape(bsz, GRID_W, rows, w).transpose(0, 2, 1, 3).reshape(bsz, s, w)


def run_chunks(step, state0, seqs):
    s = seqs[0].shape[2]
    nc = s // CHUNK
    split = lambda a: jnp.moveaxis(a.reshape(a.shape[:2] + (nc, CHUNK) + a.shape[3:]), 2, 0)
    state, ys = lax.scan(step, state0, tuple(split(a) for a in seqs))
    ys = jnp.moveaxis(ys, 0, 2)
    return ys.reshape(ys.shape[:2] + (s,) + ys.shape[4:]), state


def bidirectional_scan(step, state0, ctx_f, ctx_b, lat_f, lat_b):
    flip = lambda seq: tuple(jnp.flip(a, axis=2) for a in seq)
    yc_f, sc_f = run_chunks(step, state0, ctx_f)
    yl_f, _ = run_chunks(step, sc_f, lat_f)
    yc_b, sc_b = run_chunks(step, state0, flip(ctx_b))
    yl_b, _ = run_chunks(step, sc_b, flip(lat_b))
    return yc_f + jnp.flip(yc_b, 2), yl_f + jnp.flip(yl_b, 2)


def mlstm_chunk(carry, inp):
    C, n, m = carry
    q, k, v, ig, lf = inp
    L = q.shape[2]
    causal = jnp.tril(jnp.ones((L, L), bool))
    b = jnp.cumsum(lf, axis=-1)
    log_d = jnp.where(causal, b[..., :, None] - b[..., None, :] + ig[..., None, :], -jnp.inf)
    log_inter = b + m[..., None]
    m_row = jnp.maximum(log_inter, jnp.max(log_d, -1))
    s = jnp.einsum('bhjd,bhsd->bhjs', q, k) * jnp.exp(log_d - m_row[..., None])
    w_inter = jnp.exp(log_inter - m_row)
    num = jnp.einsum('bhjs,bhsv->bhjv', s, v) + w_inter[..., None] * jnp.einsum('bhjd,bhdv->bhjv', q, C)
    den = jnp.sum(s, -1) + w_inter * jnp.einsum('bhjd,bhd->bhj', q, n)
    h = num / jnp.maximum(jnp.abs(den), jnp.exp(-m_row))[..., None]
    b_end = b[..., -1]
    log_end = b_end[..., None] - b + ig
    m_new = jnp.maximum(b_end + m, jnp.max(log_end, -1))
    w_k = jnp.exp(log_end - m_new[..., None])
    decay = jnp.exp(b_end + m - m_new)
    C_new = decay[..., None, None] * C + jnp.einsum('bhs,bhsd,bhsv->bhdv', w_k, k, v)
    n_new = decay[..., None] * n + jnp.einsum('bhs,bhsd->bhd', w_k, k)
    return (C_new, n_new, m_new), h


def gla_chunk(S, inp):
    q, k, v, lg = inp
    L = q.shape[2]
    causal = jnp.tril(jnp.ones((L, L), bool))
    b = jnp.cumsum(lg, axis=2)
    diff = jnp.where(causal[:, :, None], b[:, :, :, None, :] - b[:, :, None, :, :], -jnp.inf)
    A = jnp.einsum('bhjd,bhsd,bhjsd->bhjs', q, k, jnp.exp(diff))
    o = jnp.einsum('bhjs,bhsv->bhjv', A, v) + jnp.einsum('bhjd,bhdv->bhjv', q * jnp.exp(b), S)
    b_end = b[:, :, -1:, :]
    S_new = jnp.exp(b_end[:, :, 0, :, None]) * S + jnp.einsum('bhsd,bhsv->bhdv', k * jnp.exp(b_end - b), v)
    return S_new, o


def mlstm_mixer(hc, hl, w_in, b_gate, conv_w, conv_b, norm_g, w_out):
    def prep(h):
        p = h @ w_in
        qk = jax.nn.silu(short_conv(p[..., :2 * A_QK], conv_w, conv_b))
        q = to_heads(qk[..., :A_QK], A_HEADS).astype(F32) * (A_DK ** -0.5)
        k = to_heads(qk[..., A_QK:], A_HEADS).astype(F32)
        v = to_heads(p[..., 2 * A_QK:2 * A_QK + A_V], A_HEADS).astype(F32)
        o = jax.nn.sigmoid(p[..., 2 * A_QK + A_V:2 * A_QK + 2 * A_V])
        g = (p[..., 2 * A_QK + 2 * A_V:] + b_gate).astype(F32)
        g = g.reshape(g.shape[0], g.shape[1], 4, A_HEADS).transpose(2, 0, 3, 1)
        fwd = (q, k, v, g[0], jax.nn.log_sigmoid(g[1]))
        bwd = (q, k, v, g[2], jax.nn.log_sigmoid(g[3]))
        return fwd, bwd, o
    cf, cb, oc = prep(hc)
    lf, lb, ol = prep(hl)
    bsz = hl.shape[0]
    state0 = (jnp.zeros((bsz, A_HEADS, A_DK, A_DV), F32), jnp.zeros((bsz, A_HEADS, A_DK), F32),
              jnp.zeros((bsz, A_HEADS), F32))
    yc, yl = bidirectional_scan(mlstm_chunk, state0, cf, cb, lf, lb)
    post = lambda y, o: (head_rmsnorm(y, norm_g) * o) @ w_out
    return post(yc, oc), post(yl, ol)


def gla_mixer(hc, hl, w_in, conv_w, conv_b, gate_w2, gate_b, norm_g, w_out):
    hl = raster_to_columns(hl)
    def prep(h):
        p = h @ w_in
        qkv = jax.nn.silu(short_conv(p[..., :2 * B_QK + B_V], conv_w, conv_b))
        q = to_heads(qkv[..., :B_QK], B_HEADS).astype(F32) * (B_DK ** -0.5)
        k = to_heads(qkv[..., B_QK:2 * B_QK], B_HEADS).astype(F32)
        v = to_heads(qkv[..., 2 * B_QK:], B_HEADS).astype(F32)
        g = p[..., 2 * B_QK + B_V:2 * B_QK + 2 * B_V]
        low = p[..., 2 * B_QK + 2 * B_V:]
        def log_alpha(d):
            pre = low[..., d * B_RANK:(d + 1) * B_RANK] @ gate_w2[d] + gate_b[d]
            return to_heads(jax.nn.log_sigmoid(pre.astype(F32)) / B_TAU, B_HEADS)
        return (q, k, v, log_alpha(0)), (q, k, v, log_alpha(1)), g
    cf, cb, gc = prep(hc)
    lf, lb, gl = prep(hl)
    state0 = jnp.zeros((hl.shape[0], B_HEADS, B_DK, B_DV), F32)
    yc, yl = bidirectional_scan(gla_chunk, state0, cf, cb, lf, lb)
    post = lambda y, g: (head_rmsnorm(y, norm_g) * jax.nn.silu(g)) @ w_out
    return post(yc, gc), columns_to_raster(post(yl, gl))


def hyena_filters(L, w1, b1, w_mid, b_mid, w_out, freq):
    t = jnp.linspace(0.0, 1.0, L, dtype=F32)[:, None]
    pos = jnp.arange(L, dtype=F32)[:, None]
    bands = jnp.linspace(1e-4, C_BANDS - 1, C_BANDS, dtype=F32)[None, :]
    ang = (2.0 * math.pi / L) * pos * bands
    feats = jnp.concatenate([t, jnp.cos(ang), -jnp.sin(ang)], -1)
    fr = freq.astype(F32)
    h = jnp.sin(fr * (feats @ w1.astype(F32) + b1.astype(F32)))
    for m in range(C_INNER):
        h = jnp.sin(fr * (h @ w_mid[m].astype(F32) + b_mid[m].astype(F32)))
    h = (h @ w_out.astype(F32)).reshape(L, 2, D_MODEL)
    deltas = jnp.abs(jnp.linspace(math.log(C_TARGET) / C_SLOW, math.log(C_TARGET) / C_FAST, D_MODEL, dtype=F32))
    h = h * jnp.exp(-t * deltas)[:, None, :]
    hf, hb = h[:, 0], h[:, 1]
    l1 = jnp.sum(jnp.abs(hf), 0) + jnp.sum(jnp.abs(hb[1:]), 0)
    return hf / l1, hb / l1


def bidir_fft_conv(z, hf, hb, skip):
    bsz, L, C = z.shape
    k = jnp.concatenate([hf, jnp.zeros((1, C), F32), hb[:0:-1]], 0)
    kf = jnp.fft.rfft(k, axis=0)
    zf = jnp.fft.rfft(z.astype(F32), n=2 * L, axis=1)
    y = jnp.fft.irfft(zf * kf[None], n=2 * L, axis=1)[:, :L]
    return (y + z.astype(F32) * skip.astype(F32)).astype(z.dtype)


def hyena_mixer(hc, hl, w_in, b_in, conv_w, conv_b, fw1, fb1, fw_mid, fb_mid, fw_out, freq, skip, w_out, b_out):
    def apply(h):
        L = h.shape[1]
        u = short_conv(h @ w_in + b_in, conv_w, conv_b)
        x0, x1, v = u[..., :D_MODEL], u[..., D_MODEL:2 * D_MODEL], u[..., 2 * D_MODEL:]
        hf, hb = hyena_filters(L, fw1, fb1, fw_mid, fb_mid, fw_out, freq)
        z = bidir_fft_conv(x1 * v, hf, hb, skip)
        return (x0 * z) @ w_out + b_out
    return apply(hc), apply(hl)


def moe_ffn(h, router_w, router_b, w1, w3, w2):
    T, D = h.shape
    scores = jax.nn.sigmoid(h.astype(F32) @ router_w.astype(F32))
    sel = (scores + router_b.astype(F32)).reshape(T, N_GROUPS, EXPERTS_PER_GROUP)
    group = jnp.argmax(jnp.sum(lax.top_k(sel, TOP_K)[0], -1), -1)
    in_group = sel[jnp.arange(T), group]
    _, local = lax.top_k(in_group, TOP_K)
    expert_ids = group[:, None] * EXPERTS_PER_GROUP + local
    gate = jnp.take_along_axis(scores, expert_ids, axis=1)
    gate = gate / jnp.sum(gate, -1, keepdims=True)
    n_assign = T * TOP_K
    e_flat = expert_ids.reshape(-1)
    tok_flat = jnp.repeat(jnp.arange(T, dtype=jnp.int32), TOP_K)
    w_flat = gate.reshape(-1)
    counts = jnp.bincount(e_flat, length=N_EXPERTS)
    padded = (counts + MOE_BLOCK - 1) // MOE_BLOCK * MOE_BLOCK
    pad_end = jnp.cumsum(padded)
    pad_start = pad_end - padded
    start = jnp.cumsum(counts) - counts
    order = jnp.argsort(e_flat)
    e_sorted = e_flat[order]
    dest = pad_start[e_sorted] + (jnp.arange(n_assign) - start[e_sorted])
    n_blocks = n_assign // MOE_BLOCK + N_EXPERTS
    n_slots = n_blocks * MOE_BLOCK
    slot_tok = jnp.full((n_slots,), T, jnp.int32).at[dest].set(tok_flat[order])
    slot_w = jnp.zeros((n_slots,), h.dtype).at[dest].set(w_flat[order].astype(h.dtype))
    block_exp = jnp.minimum(jnp.searchsorted(pad_end, jnp.arange(n_blocks) * MOE_BLOCK, side='right'), N_EXPERTS - 1)
    h_pad = jnp.concatenate([h, jnp.zeros((1, D), h.dtype)], 0)
    xb = h_pad[slot_tok].reshape(n_blocks, MOE_BLOCK, D)
    def expert_block(args):
        xblk, e = args
        return (jax.nn.silu(xblk @ w1[e]) * (xblk @ w3[e])) @ w2[e]
    yb = lax.map(expert_block, (xb, block_exp)).reshape(n_slots, D)
    out = jnp.zeros((T + 1, D), h.dtype).at[slot_tok].add(yb * slot_w[:, None])
    return out[:T]


def setup_inputs(seed: int = 0) -> dict:
    key = jax.random.key(seed)
    ks = iter(jax.random.split(key, 64))
    nrm = lambda shape, scale: jax.random.normal(next(ks), shape, F32) * scale
    D = D_MODEL
    nA, nB, nC = N_LAYERS_A, N_LAYERS_B, N_LAYERS_C
    f_bias = jnp.linspace(3.0, 6.0, A_HEADS, dtype=F32)
    zh = jnp.zeros((A_HEADS,), F32)
    gate_base = jnp.concatenate([zh, f_bias, zh, f_bias])
    return {
        'x': nrm((BATCH, SEQ, D), 1.0),
        'c': nrm((BATCH, D), 1.0),
        'ctx': nrm((BATCH, CTX_LEN, D), 1.0),
        'c_ctx': nrm((D,), 1.0),
        'ada_w': nrm((DEPTH, D, 6 * D), 0.5 * D ** -0.5),
        'ada_b': nrm((DEPTH, 6 * D), 0.02),
        'norm1_g': 1.0 + nrm((DEPTH, D), 0.02),
        'norm2_g': 1.0 + nrm((DEPTH, D), 0.02),
        'final_g': 1.0 + nrm((D,), 0.02),
        'ml_w_in': nrm((nA, D, A_IN), D ** -0.5),
        'ml_b_gate': gate_base + nrm((nA, 4 * A_HEADS), 0.1),
        'ml_conv_w': nrm((nA, CONV_W, 2 * A_QK), CONV_W ** -0.5),
        'ml_conv_b': nrm((nA, 2 * A_QK), 0.02),
        'ml_norm_g': 1.0 + nrm((nA, A_V), 0.02),
        'ml_w_out': nrm((nA, A_V, D), A_V ** -0.5),
        'gla_w_in': nrm((nB, D, B_IN), D ** -0.5),
        'gla_conv_w': nrm((nB, CONV_W, 2 * B_QK + B_V), CONV_W ** -0.5),
        'gla_conv_b': nrm((nB, 2 * B_QK + B_V), 0.02),
        'gla_gate_w2': nrm((nB, 2, B_RANK, B_QK), B_RANK ** -0.5),
        'gla_gate_b': nrm((nB, 2, B_QK), 0.1),
        'gla_norm_g': 1.0 + nrm((nB, B_V), 0.02),
        'gla_w_out': nrm((nB, B_V, D), B_V ** -0.5),
        'hy_w_in': nrm((nC, D, 3 * D), D ** -0.5),
        'hy_b_in': nrm((nC, 3 * D), 0.02),
        'hy_conv_w': nrm((nC, CONV_W, 3 * D), CONV_W ** -0.5),
        'hy_conv_b': nrm((nC, 3 * D), 0.02),
        'hy_filt_w1': nrm((nC, C_EMB, C_FF), C_EMB ** -0.5),
        'hy_filt_b1': nrm((nC, C_FF), 0.02),
        'hy_filt_w_mid': nrm((nC, C_INNER, C_FF, C_FF), C_FF ** -0.5),
        'hy_filt_b_mid': nrm((nC, C_INNER, C_FF), 0.02),
        'hy_filt_w_out': nrm((nC, C_FF, 2 * D), C_FF ** -0.5),
        'hy_filt_freq': 1.0 + nrm((nC, C_FF), 0.1),
        'hy_skip': nrm((nC, D), 1.0),
        'hy_w_out': nrm((nC, D, D), D ** -0.5),
        'hy_b_out': nrm((nC, D), 0.02),
        'router_w': nrm((D, N_EXPERTS), D ** -0.5),
        'router_b': nrm((N_EXPERTS,), 0.01),
        'exp_w1': nrm((DEPTH, N_EXPERTS, D, EXPERT_FF), D ** -0.5),
        'exp_w3': nrm((DEPTH, N_EXPERTS, D, EXPERT_FF), D ** -0.5),
        'exp_w2': nrm((DEPTH, N_EXPERTS, EXPERT_FF, D), EXPERT_FF ** -0.5),
    }


def reference(x, c, ctx, c_ctx, ada_w, ada_b, norm1_g, norm2_g, final_g,
              ml_w_in, ml_b_gate, ml_conv_w, ml_conv_b, ml_norm_g, ml_w_out,
              gla_w_in, gla_conv_w, gla_conv_b, gla_gate_w2, gla_gate_b, gla_norm_g, gla_w_out,
              hy_w_in, hy_b_in, hy_conv_w, hy_conv_b, hy_filt_w1, hy_filt_b1, hy_filt_w_mid,
              hy_filt_b_mid, hy_filt_w_out, hy_filt_freq, hy_skip, hy_w_out, hy_b_out,
              router_w, router_b, exp_w1, exp_w3, exp_w2):
    bsz, s, D = x.shape
    xc = ctx
    n_ctx_tok = bsz * ctx.shape[1]
    silu_c = jax.nn.silu(c)
    silu_cc = jax.nn.silu(c_ctx)
    for i in range(DEPTH):
        last = i == DEPTH - 1
        mod = silu_c @ ada_w[i] + ada_b[i]
        mod_c = silu_cc @ ada_w[i] + ada_b[i]
        sh1, sc1, g1, sh2, sc2, g2 = jnp.split(mod[:, None, :], 6, axis=-1)
        sh1c, sc1c, g1c, sh2c, sc2c, g2c = jnp.split(mod_c, 6, axis=-1)
        hl = rmsnorm(x, norm1_g[i]) * (1.0 + sc1) + sh1
        hc = rmsnorm(xc, norm1_g[i]) * (1.0 + sc1c) + sh1c
        kind, j = i % N_MIXERS, i // N_MIXERS
        if kind == 0:
            yc, yl = mlstm_mixer(hc, hl, ml_w_in[j], ml_b_gate[j], ml_conv_w[j], ml_conv_b[j],
                                 ml_norm_g[j], ml_w_out[j])
        elif kind == 1:
            yc, yl = gla_mixer(hc, hl, gla_w_in[j], gla_conv_w[j], gla_conv_b[j], gla_gate_w2[j],
                               gla_gate_b[j], gla_norm_g[j], gla_w_out[j])
        else:
            yc, yl = hyena_mixer(hc, hl, hy_w_in[j], hy_b_in[j], hy_conv_w[j], hy_conv_b[j],
                                 hy_filt_w1[j], hy_filt_b1[j], hy_filt_w_mid[j], hy_filt_b_mid[j],
                                 hy_filt_w_out[j], hy_filt_freq[j], hy_skip[j], hy_w_out[j], hy_b_out[j])
        x = x + g1 * yl
        hl = rmsnorm(x, norm2_g[i]) * (1.0 + sc2) + sh2
        if last:
            x = x + g2 * moe_ffn(hl.reshape(-1, D), router_w, router_b, exp_w1[i], exp_w3[i], exp_w2[i]).reshape(bsz, s, D)
        else:
            xc = xc + g1c * yc
            hc = rmsnorm(xc, norm2_g[i]) * (1.0 + sc2c) + sh2c
            tokens = jnp.concatenate([hc.reshape(-1, D), hl.reshape(-1, D)], 0)
            y = moe_ffn(tokens, router_w, router_b, exp_w1[i], exp_w3[i], exp_w2[i])
            xc = xc + g2c * y[:n_ctx_tok].reshape(xc.shape)
            x = x + g2 * y[n_ctx_tok:].reshape(bsz, s, D)
    return rmsnorm(x, final_g)
```

```python
import functools
import math

import numpy as np
import jax
import jax.numpy as jnp
from jax import lax
from jax.experimental import pallas as pl
from jax.experimental.pallas import tpu as pltpu

F32 = jnp.float32
BF16 = jnp.bfloat16
I32 = jnp.int32

EPS = 1e-6
TM = 256
LC = 256
MOE_BLOCK = 512
N_EXPERTS = 16
N_GROUPS = 4
EPG = N_EXPERTS // N_GROUPS
A_HEADS, A_DK, A_DV = 8, 64, 128
B_HEADS, B_DK, B_DV = 4, 128, 256
B_RANK = 16
B_TAU = 16.0
GRID_W = 64
VMEM_LIMIT_V7X = 56 * 1024 * 1024


def _cparams(*sem):
    return pltpu.CompilerParams(dimension_semantics=sem, vmem_limit_bytes=VMEM_LIMIT_V7X)


def _silu(x):
    return x * (1.0 / (1.0 + jnp.exp(-x)))


def _sigmoid(x):
    return 1.0 / (1.0 + jnp.exp(-x))


def _log_sigmoid(x):
    return jnp.minimum(x, 0.0) - jnp.log(1.0 + jnp.exp(-jnp.abs(x)))


def _split3(x):
    hi = x.astype(BF16)
    r1 = x - hi.astype(F32)
    mid = r1.astype(BF16)
    lo = (r1 - mid.astype(F32)).astype(BF16)
    return hi, mid, lo


def _dot(a, b):
    return jnp.dot(a, b, preferred_element_type=F32)


def _dot_nt(a, b):
    return lax.dot_general(a, b, (((1,), (1,)), ((), ())), preferred_element_type=F32)


def _dot_tn(a, b):
    return lax.dot_general(a, b, (((0,), (0,)), ((), ())), preferred_element_type=F32)


def _dot_exact_lhs01(m01, x):
    hi, mid, lo = _split3(x)
    return (_dot(m01, hi) + _dot(m01, mid)) + _dot(m01, lo)


def _dot_exact_rhs01(x, m01):
    hi, mid, lo = _split3(x)
    return (_dot(hi, m01) + _dot(mid, m01)) + _dot(lo, m01)


def _ada_kernel(s_ref, w_ref, b_ref, o_ref):
    s = _silu(s_ref[...]).astype(BF16)
    o_ref[0] = _dot(s, w_ref[0].astype(BF16)) + b_ref[0]


def ada_mods(rows, ada_w, ada_b):
    depth, d, n6 = ada_w.shape
    nr = rows.shape[0]
    tn = 1536
    return pl.pallas_call(
        _ada_kernel,
        grid=(depth, n6 // tn),
        in_specs=[pl.BlockSpec((nr, d), lambda l, j: (0, 0)),
                  pl.BlockSpec((1, d, tn), lambda l, j: (l, 0, j)),
                  pl.BlockSpec((1, 1, tn), lambda l, j: (l, 0, j))],
        out_specs=pl.BlockSpec((1, nr, tn), lambda l, j: (l, 0, j)),
        out_shape=jax.ShapeDtypeStruct((depth, nr, n6), F32),
        compiler_params=_cparams("parallel", "parallel"),
        name="ada_mods",
    )(rows, ada_w, ada_b.reshape(depth, 1, n6))


def _norm_mod(x, g, shift, scale):
    y = x * lax.rsqrt(jnp.mean(x * x, axis=-1, keepdims=True) + EPS) * g
    return y * (1.0 + scale) + shift


def _in_proj_kernel(x_ref, mod_ref, g_ref, w_ref, b_ref, p_ref, *rest, n_main):
    h = _norm_mod(x_ref[...], g_ref[...], mod_ref[0, 0:1, :], mod_ref[0, 1:2, :]).astype(BF16)
    p_ref[...] = (_dot(h, w_ref[:, :n_main]) + b_ref[:, :n_main]).astype(p_ref.dtype)
    if rest:
        rest[0][...] = _dot(h, w_ref[:, n_main:]) + b_ref[:, n_main:]


def in_proj(x, mods_l, g, w, b, n_main, tiles_per_batch, p_dtype):
    t, d = x.shape
    n = w.shape[1]
    n_extra = n - n_main
    mod_row = lambda i: jnp.where(i % tiles_per_batch == 0, 0, 1 + i // tiles_per_batch)
    out_shape = [jax.ShapeDtypeStruct((t, n_main), p_dtype)]
    out_specs = [pl.BlockSpec((TM, n_main), lambda i: (i, 0))]
    if n_extra:
        out_shape.append(jax.ShapeDtypeStruct((t, n_extra), F32))
        out_specs.append(pl.BlockSpec((TM, n_extra), lambda i: (i, 0)))
    return pl.pallas_call(
        functools.partial(_in_proj_kernel, n_main=n_main),
        grid=(t // TM,),
        in_specs=[pl.BlockSpec((TM, d), lambda i: (i, 0)),
                  pl.BlockSpec((1, 6, d), lambda i: (mod_row(i), 0, 0)),
                  pl.BlockSpec((1, d), lambda i: (0, 0)),
                  pl.BlockSpec((d, n), lambda i: (0, 0)),
                  pl.BlockSpec((1, n), lambda i: (0, 0))],
        out_specs=out_specs,
        out_shape=out_shape,
        compiler_params=_cparams("parallel"),
        name="in_proj",
    )(x, mods_l, g.reshape(1, d), w, b.reshape(1, n))


def _first_argmax(vals):
    best, idx = vals[0], jnp.zeros(vals[0].shape, I32)
    for i in range(1, len(vals)):
        take = vals[i] > best
        best = jnp.where(take, vals[i], best)
        idx = jnp.where(take, i, idx)
    return best, idx


def _select_row(rows, idx):
    out = rows[0]
    for i in range(1, len(rows)):
        out = jnp.where(idx == i, rows[i], out)
    return out


def _route(scores, sel):
    neg = jnp.float32(-jnp.inf)
    srow = [sel[e:e + 1, :] for e in range(N_EXPERTS)]
    crow = [scores[e:e + 1, :] for e in range(N_EXPERTS)]
    gscore = []
    for g in range(N_GROUPS):
        a = srow[g * EPG:(g + 1) * EPG]
        m1, i1 = _first_argmax(a)
        m2, _ = _first_argmax([jnp.where(i1 == i, neg, a[i]) for i in range(EPG)])
        gscore.append(m1 + m2)
    _, grp = _first_argmax(gscore)
    in_sel = [_select_row([srow[g * EPG + i] for g in range(N_GROUPS)], grp) for i in range(EPG)]
    in_sc = [_select_row([crow[g * EPG + i] for g in range(N_GROUPS)], grp) for i in range(EPG)]
    _, l1 = _first_argmax(in_sel)
    _, l2 = _first_argmax([jnp.where(l1 == i, neg, in_sel[i]) for i in range(EPG)])
    g1 = _select_row(in_sc, l1)
    g2 = _select_row(in_sc, l2)
    tot = g1 + g2
    ids = jnp.concatenate([grp * EPG + l1, grp * EPG + l2], axis=0)
    gates = jnp.concatenate([g1 / tot, g2 / tot], axis=0)
    return ids, gates


def _out_route_kernel(x_ref, y_ref, mod_ref, w_ref, b_ref, g_ref, rw_ref, rb_ref,
                      xo_ref, h_ref, ids_ref, gates_ref, rank_ref, cnt_ref, carry_ref):
    i = pl.program_id(0)

    @pl.when(i == 0)
    def _():
        carry_ref[...] = jnp.zeros_like(carry_ref)

    proj = _dot(y_ref[...], w_ref[...]) + b_ref[...]
    xn = x_ref[...] + mod_ref[0, 2:3, :] * proj
    xo_ref[...] = xn
    h = _norm_mod(xn, g_ref[...], mod_ref[0, 3:4, :], mod_ref[0, 4:5, :])
    h_ref[...] = h.astype(BF16)

    logits = lax.dot_general(rw_ref[...], h, (((1,), (1,)), ((), ())),
                             precision=lax.Precision.HIGHEST, preferred_element_type=F32)
    scores = _sigmoid(logits)
    ids, gates = _route(scores, scores + rb_ref[:, 0:1])
    ids_ref[...] = ids
    gates_ref[...] = gates

    n = ids.shape[1]
    erow = lax.broadcasted_iota(I32, (N_EXPERTS, n), 0)
    oh0 = (erow == ids[0:1, :]).astype(F32)
    oh1 = (erow == ids[1:2, :]).astype(F32)
    before = (lax.broadcasted_iota(I32, (n, n), 0) < lax.broadcasted_iota(I32, (n, n), 1)).astype(BF16)
    p0 = _dot(oh0.astype(BF16), before)
    p1 = _dot(oh1.astype(BF16), before)
    tot0 = jnp.sum(oh0, axis=1, keepdims=True)
    tot1 = jnp.sum(oh1, axis=1, keepdims=True)
    carry = carry_ref[:, 0:1]
    r0 = jnp.sum(oh0 * (carry + p0), axis=0, keepdims=True)
    r1 = jnp.sum(oh1 * (carry + tot0 + p1), axis=0, keepdims=True)
    rank_ref[...] = jnp.concatenate([r0, r1], axis=0).astype(I32)
    new = carry + tot0 + tot1
    carry_ref[...] = jnp.broadcast_to(new, carry_ref.shape)
    cnt_ref[...] = jnp.broadcast_to(new, cnt_ref.shape).astype(I32)


def out_route(x, y, mods_l, w_out, b_out, g2n, router_wt, router_b, tiles_per_batch, ctx_tiles, skip_ctx):
    t, d = x.shape
    nb = t // (tiles_per_batch * TM)
    if skip_ctx:
        per = tiles_per_batch - ctx_tiles
        src = lambda i: (i // per) * tiles_per_batch + ctx_tiles + i % per
        mod_row = lambda i: 1 + i // per
        n_tiles = nb * per
    else:
        src = lambda i: i
        mod_row = lambda i: jnp.where(i % tiles_per_batch < ctx_tiles, 0, 1 + i // tiles_per_batch)
        n_tiles = nb * tiles_per_batch
    tr = n_tiles * TM
    tok = lambda i: (i, 0)
    row2 = lambda i: (0, i)
    outs = pl.pallas_call(
        _out_route_kernel,
        grid=(n_tiles,),
        in_specs=[pl.BlockSpec((TM, d), lambda i: (src(i), 0)),
                  pl.BlockSpec((TM, d), lambda i: (src(i), 0)),
                  pl.BlockSpec((1, 6, d), lambda i: (mod_row(i), 0, 0)),
                  pl.BlockSpec((d, d), lambda i: (0, 0)),
                  pl.BlockSpec((1, d), lambda i: (0, 0)),
                  pl.BlockSpec((1, d), lambda i: (0, 0)),
                  pl.BlockSpec((N_EXPERTS, d), lambda i: (0, 0)),
                  pl.BlockSpec((N_EXPERTS, 128), lambda i: (0, 0))],
        out_specs=[pl.BlockSpec((TM, d), tok),
                   pl.BlockSpec((TM, d), tok),
                   pl.BlockSpec((2, TM), row2),
                   pl.BlockSpec((2, TM), row2),
                   pl.BlockSpec((2, TM), row2),
                   pl.BlockSpec((N_EXPERTS, 128), lambda i: (0, 0))],
        out_shape=[jax.ShapeDtypeStruct((tr, d), F32),
                   jax.ShapeDtypeStruct((tr, d), BF16),
                   jax.ShapeDtypeStruct((2, tr), I32),
                   jax.ShapeDtypeStruct((2, tr), F32),
                   jax.ShapeDtypeStruct((2, tr), I32),
                   jax.ShapeDtypeStruct((N_EXPERTS, 128), I32)],
        scratch_shapes=[pltpu.VMEM((N_EXPERTS, 128), F32)],
        compiler_params=_cparams("arbitrary"),
        name="out_route",
    )(x, y, mods_l, w_out, b_out.reshape(1, d), g2n.reshape(1, d), router_wt,
      jnp.broadcast_to(router_b.reshape(N_EXPERTS, 1), (N_EXPERTS, 128)))
    return outs


def _moe_kernel(bexp_ref, nused_ref, x_ref, w1_ref, w3_ref, w2_ref, o_ref, w1b, w3b, w2b):
    i = pl.program_id(0)
    prev = bexp_ref[jnp.maximum(i - 1, 0)]
    fresh = jnp.logical_or(i == 0, bexp_ref[i] != prev)
    used = i < nused_ref[0]

    @pl.when(jnp.logical_and(fresh, used))
    def _():
        w1b[...] = w1_ref[0].astype(BF16)
        w3b[...] = w3_ref[0].astype(BF16)
        w2b[...] = w2_ref[0].astype(BF16)

    @pl.when(used)
    def _():
        x = x_ref[...]
        a = _dot(x, w1b[...])
        b = _dot(x, w3b[...])
        hmid = (_silu(a) * b).astype(BF16)
        o_ref[...] = _dot(hmid, w2b[...]).astype(o_ref.dtype)

    @pl.when(jnp.logical_not(used))
    def _():
        o_ref[...] = jnp.zeros_like(o_ref)


def moe_experts(xb, block_exp, n_used, w1, w3, w2, layer):
    n_slots, d = xb.shape
    ff = w1.shape[3]
    n_blocks = n_slots // MOE_BLOCK
    wmap = lambda i, be, nu: (layer, be[i], 0, 0)
    return pl.pallas_call(
        _moe_kernel,
        grid_spec=pltpu.PrefetchScalarGridSpec(
            num_scalar_prefetch=2,
            grid=(n_blocks,),
            in_specs=[pl.BlockSpec((MOE_BLOCK, d), lambda i, be, nu: (i, 0)),
                      pl.BlockSpec((None, 1, d, ff), wmap),
                      pl.BlockSpec((None, 1, d, ff), wmap),
                      pl.BlockSpec((None, 1, ff, d), wmap)],
            out_specs=pl.BlockSpec((MOE_BLOCK, d), lambda i, be, nu: (i, 0)),
            scratch_shapes=[pltpu.VMEM((d, ff), BF16), pltpu.VMEM((d, ff), BF16),
                            pltpu.VMEM((ff, d), BF16)]),
        out_shape=jax.ShapeDtypeStruct((n_slots, d), BF16),
        compiler_params=_cparams("arbitrary"),
        name="moe_experts",
    )(block_exp, n_used, xb, w1, w3, w2)


def moe_plan(ids, ranks, counts, n_tokens):
    n_assign = 2 * n_tokens
    n_blocks = n_assign // MOE_BLOCK + N_EXPERTS
    cnt = counts[:, 0]
    padded = (cnt + MOE_BLOCK - 1) // MOE_BLOCK * MOE_BLOCK
    pad_end = jnp.cumsum(padded)
    pad_start = pad_end - padded
    dest = pad_start[ids] + ranks
    blk_start = jnp.arange(n_blocks, dtype=I32) * MOE_BLOCK
    block_exp = jnp.minimum(jnp.sum(blk_start[:, None] >= pad_end[None, :], axis=1), N_EXPERTS - 1).astype(I32)
    n_used = (pad_end[-1] // MOE_BLOCK).astype(I32).reshape(1)
    tok = jnp.broadcast_to(jnp.arange(n_tokens, dtype=I32)[None, :], (2, n_tokens))
    slot_tok = jnp.zeros((n_blocks * MOE_BLOCK,), I32).at[dest.reshape(-1)].set(tok.reshape(-1))
    return dest, slot_tok, block_exp, n_used


def _combine_kernel(x_ref, y0_ref, y1_ref, gt_ref, mod_ref, *rest, final):
    moe = gt_ref[:, 0:1] * y0_ref[...].astype(F32) + gt_ref[:, 1:2] * y1_ref[...].astype(F32)
    xn = x_ref[...] + mod_ref[0, 5:6, :] * moe
    if final:
        g_ref, o_ref = rest
        o_ref[...] = xn * lax.rsqrt(jnp.mean(xn * xn, axis=-1, keepdims=True) + EPS) * g_ref[...]
    else:
        rest[0][...] = xn


def moe_combine(x, y0, y1, gates_t, mods_l, mod_row, final_g=None):
    t, d = x.shape
    tok = lambda i: (i, 0)
    in_specs = [pl.BlockSpec((TM, d), tok), pl.BlockSpec((TM, d), tok), pl.BlockSpec((TM, d), tok),
                pl.BlockSpec((TM, 2), tok), pl.BlockSpec((1, 6, d), lambda i: (mod_row(i), 0, 0))]
    args = [x, y0, y1, gates_t, mods_l]
    if final_g is not None:
        in_specs.append(pl.BlockSpec((1, d), lambda i: (0, 0)))
        args.append(final_g.reshape(1, d))
    return pl.pallas_call(
        functools.partial(_combine_kernel, final=final_g is not None),
        grid=(t // TM,),
        in_specs=in_specs,
        out_specs=pl.BlockSpec((TM, d), tok),
        out_shape=jax.ShapeDtypeStruct((t, d), F32),
        compiler_params=_cparams("parallel"),
        name="moe_combine",
    )(*args)


def _conv3(x_ref, cw_ref, cb_ref, n_rows):
    x = x_ref[...].astype(F32)
    row = lax.broadcasted_iota(I32, (n_rows, 1), 0)
    first = jnp.logical_or(row == 0, row == LC)
    last = jnp.logical_or(row == LC - 1, row == n_rows - 1)
    prev = jnp.where(first, 0.0, pltpu.roll(x, 1, 0))
    nxt = jnp.where(last, 0.0, pltpu.roll(x, n_rows - 1, 0))
    return prev * cw_ref[0:1, :] + x * cw_ref[1:2, :] + nxt * cw_ref[2:3, :] + cb_ref[...]


def _conv_silu(x_ref, cw_ref, cb_ref, n_rows):
    return _silu(_conv3(x_ref, cw_ref, cb_ref, n_rows))


def _tri_masks():
    ri = lax.broadcasted_iota(I32, (LC, LC), 0)
    ci = lax.broadcasted_iota(I32, (LC, LC), 1)
    return ri >= ci, ri <= ci


def _scan_schedule(n_chunks, step):
    step(0, 0)

    def body(t, carry):
        step(t, n_chunks - t)
        return carry
    lax.fori_loop(1, n_chunks, body, 0)


def _mlstm_kernel(q_ref, k_ref, v_ref, o_ref, g_ref, gt_ref, cwq_ref, cwk_ref, cbq_ref, cbk_ref, ng_ref,
                  y_ref, qs, ks, yf, yb, c_scr, m_scr, *, n_chunks):
    n_rows = n_chunks * LC
    qs[...] = (_conv_silu(q_ref, cwq_ref, cbq_ref, n_rows) * (A_DK ** -0.5)).astype(BF16)
    ks[...] = _conv_silu(k_ref, cwk_ref, cbk_ref, n_rows).astype(BF16)
    c_scr[...] = jnp.zeros_like(c_scr)
    m_scr[...] = jnp.zeros_like(m_scr)
    tril, triu = _tri_masks()
    tril_b, triu_b = tril.astype(BF16), triu.astype(BF16)
    ones_col = (lax.broadcasted_iota(I32, (LC, A_DV), 1) == 0).astype(BF16)
    neg = jnp.float32(-jnp.inf)

    def chunk(c, fwd):
        off = pl.multiple_of(c * LC, LC)
        rows = pl.ds(off, LC)
        g = g_ref[0, rows, :]
        gt = gt_ref[0, 0, c]
        if fwd:
            cs_col = _dot_exact_lhs01(tril_b, _log_sigmoid(g))
            cs_row = _dot_exact_rhs01(_log_sigmoid(gt), triu_b)
        else:
            cs_col = _dot_exact_lhs01(triu_b, _log_sigmoid(g))
            cs_row = _dot_exact_rhs01(_log_sigmoid(gt), tril_b)
        mask = tril if fwd else triu
        ydst = yf if fwd else yb
        d = 0 if fwd else 1
        for hh in range(2):
            ic, fc, sidx = 4 * d + hh, 4 * d + 2 + hh, 2 * d + hh
            ig_c, f_c = g[:, ic:ic + 1], cs_col[:, fc:fc + 1]
            ig_r, f_r = gt[ic:ic + 1, :], cs_row[fc:fc + 1, :]
            m_prev = m_scr[sidx, 0:1, 0:1]
            log_d = jnp.where(mask, f_c - f_r + ig_r, neg)
            li = f_c + m_prev
            m_row = jnp.maximum(li, jnp.max(log_d, axis=1, keepdims=True))
            dm = jnp.exp(log_d - m_row)
            qh = qs[rows, hh * A_DK:(hh + 1) * A_DK]
            kh = ks[rows, hh * A_DK:(hh + 1) * A_DK]
            vext = jnp.concatenate([v_ref[rows, hh * A_DV:(hh + 1) * A_DV].astype(BF16), ones_col], axis=1)
            sm = _dot_nt(qh, kh) * dm
            cext = c_scr[sidx]
            num = _dot(sm.astype(BF16), vext) + jnp.exp(li - m_row) * _dot(qh, cext.astype(BF16))
            den = num[:, A_DV:A_DV + 1]
            ydst[rows, hh * A_DV:(hh + 1) * A_DV] = num[:, :A_DV] / jnp.maximum(jnp.abs(den), jnp.exp(-m_row))
            b_end = f_r[:, LC - 1:LC] if fwd else f_r[:, 0:1]
            m_new = jnp.maximum(b_end + m_prev, jnp.max(b_end - f_r + ig_r, axis=1, keepdims=True))
            wk = jnp.exp(b_end - f_c + ig_c - m_new)
            kw = (kh.astype(F32) * wk).astype(BF16)
            c_scr[sidx] = jnp.exp(b_end + m_prev - m_new) * cext + _dot_tn(kw, vext)
            m_scr[sidx] = jnp.broadcast_to(m_new, m_scr.shape[1:])

    def step(cf, cb):
        chunk(cf, True)
        chunk(cb, False)
    _scan_schedule(n_chunks, step)

    y = yf[...] + yb[...]
    for hh in range(2):
        sl = slice(hh * A_DV, (hh + 1) * A_DV)
        yh = y[:, sl]
        yn = yh * lax.rsqrt(jnp.mean(yh * yh, axis=-1, keepdims=True) + EPS)
        y_ref[:, sl] = (yn * ng_ref[:, sl] * _sigmoid(o_ref[:, sl].astype(F32))).astype(y_ref.dtype)


def mlstm_scan(p, g_cols, g_rows, conv_w, conv_b, norm_g, n_batch, n_chunks):
    rows = n_chunks * LC
    n_hp = A_HEADS // 2
    qk_w = 2 * A_DK
    v_w = 2 * A_DV
    kq, kv, ko = (A_HEADS * A_DK) // qk_w, (2 * A_HEADS * A_DK) // v_w, (2 * A_HEADS * A_DK + A_HEADS * A_DV) // v_w
    return pl.pallas_call(
        functools.partial(_mlstm_kernel, n_chunks=n_chunks),
        grid=(n_batch, n_hp),
        in_specs=[pl.BlockSpec((rows, qk_w), lambda b, h: (b, h)),
                  pl.BlockSpec((rows, qk_w), lambda b, h: (b, kq + h)),
                  pl.BlockSpec((rows, v_w), lambda b, h: (b, kv + h)),
                  pl.BlockSpec((rows, v_w), lambda b, h: (b, ko + h)),
                  pl.BlockSpec((1, rows, 8), lambda b, h: (h, b, 0)),
                  pl.BlockSpec((1, 1, n_chunks, 8, LC), lambda b, h: (h, b, 0, 0, 0)),
                  pl.BlockSpec((3, qk_w), lambda b, h: (0, h)),
                  pl.BlockSpec((3, qk_w), lambda b, h: (0, kq + h)),
                  pl.BlockSpec((1, qk_w), lambda b, h: (0, h)),
                  pl.BlockSpec((1, qk_w), lambda b, h: (0, kq + h)),
                  pl.BlockSpec((1, v_w), lambda b, h: (0, h))],
        out_specs=pl.BlockSpec((rows, v_w), lambda b, h: (b, h)),
        out_shape=jax.ShapeDtypeStruct((n_batch * rows, A_HEADS * A_DV), BF16),
        scratch_shapes=[pltpu.VMEM((rows, qk_w), BF16), pltpu.VMEM((rows, qk_w), BF16),
                        pltpu.VMEM((rows, v_w), F32), pltpu.VMEM((rows, v_w), F32),
                        pltpu.VMEM((4, A_DK, 2 * A_DV), F32), pltpu.VMEM((4, 8, 128), F32)],
        compiler_params=_cparams("parallel", "parallel"),
        name="mlstm_scan",
    )(p, p, p, p, g_cols, g_rows, conv_w, conv_w, conv_b, conv_b, norm_g)


GLA_LEVELS = int(math.log2(LC))


def _gla_level_masks():
    j = np.arange(LC)[:, None]
    s = np.arange(LC)[None, :]
    out = []
    for fwd in (True, False):
        for lev in range(GLA_LEVELS):
            m = 1 << lev
            same = (j // (2 * m)) == (s // (2 * m))
            hi_j, hi_s = (j % (2 * m)) >= m, (s % (2 * m)) >= m
            out.append(same & (hi_j & ~hi_s if fwd else ~hi_j & hi_s))
    return np.stack(out).astype(np.float32)


def _seg_ref(b, m, fwd):
    n, w = b.shape
    r = m - 1 if fwd else m
    if 2 * m >= 8:
        b3 = b.reshape(n // (2 * m), 2 * m, w)
        return jnp.broadcast_to(b3[:, r:r + 1, :], b3.shape).reshape(n, w)
    b3 = b.reshape(n // 8, 8, w)
    sub = lax.broadcasted_iota(I32, b3.shape, 1)
    out = None
    for blk in range(8 // (2 * m)):
        row = jnp.broadcast_to(b3[:, blk * 2 * m + r:blk * 2 * m + r + 1, :], b3.shape)
        out = row if out is None else jnp.where(sub >= blk * 2 * m, row, out)
    return out.reshape(n, w)


def _gla_kernel(q_ref, k_ref, v_ref, g_ref, low_ref, cwq_ref, cwk_ref, cwv_ref, cbq_ref, cbk_ref, cbv_ref,
                gw_ref, gb_ref, ng_ref, mask_ref, y_ref, qs, ks, vs, yf, yb, st_scr, *, n_chunks):
    n_rows = n_chunks * LC
    qs[...] = (_conv_silu(q_ref, cwq_ref, cbq_ref, n_rows) * (B_DK ** -0.5)).astype(BF16)
    ks[...] = _conv_silu(k_ref, cwk_ref, cbk_ref, n_rows).astype(BF16)
    vs[...] = _conv_silu(v_ref, cwv_ref, cbv_ref, n_rows).astype(BF16)
    st_scr[...] = jnp.zeros_like(st_scr)
    tril, triu = _tri_masks()
    tril_b, triu_b = tril.astype(BF16), triu.astype(BF16)
    eye = jnp.logical_and(tril, triu).astype(F32)

    def chunk(c, fwd):
        off = pl.multiple_of(c * LC, LC)
        rows = pl.ds(off, LC)
        d = 0 if fwd else 1
        pre = jnp.dot(low_ref[rows, :], gw_ref[d], precision=lax.Precision.HIGHEST,
                      preferred_element_type=F32) + gb_ref[d:d + 1, :]
        lg = _log_sigmoid(pre) * (1.0 / B_TAU)
        b = _dot_exact_lhs01(tril_b if fwd else triu_b, lg)
        q = qs[rows, :].astype(F32)
        k = ks[rows, :].astype(F32)
        v = vs[rows, :]
        a = eye * jnp.sum(q * k, axis=-1, keepdims=True)
        for lev in range(GLA_LEVELS):
            bref = _seg_ref(b, 1 << lev, fwd)
            qt = (q * jnp.exp(jnp.minimum(b - bref, 0.0))).astype(BF16)
            kt = (k * jnp.exp(jnp.minimum(bref - b, 0.0))).astype(BF16)
            a = a + mask_ref[d * GLA_LEVELS + lev].astype(F32) * _dot_nt(qt, kt)
        st = st_scr[d]
        o = _dot(a.astype(BF16), v) + _dot_nt((q * jnp.exp(b)).astype(BF16), st.astype(BF16))
        (yf if fwd else yb)[rows, :] = o
        b_end = b[LC - 1:LC, :] if fwd else b[0:1, :]
        kdec = (k * jnp.exp(b_end - b)).astype(BF16)
        st_scr[d] = st * jnp.exp(b_end) + _dot_tn(v, kdec)

    def step(cf, cb):
        chunk(cf, True)
        chunk(cb, False)
    _scan_schedule(n_chunks, step)

    y = yf[...] + yb[...]
    yn = y * lax.rsqrt(jnp.mean(y * y, axis=-1, keepdims=True) + EPS)
    y_ref[...] = (yn * ng_ref[...] * _silu(g_ref[...].astype(F32))).astype(y_ref.dtype)


def gla_scan(p, low, conv_w, conv_b, gate_w, gate_b, norm_g, n_batch, n_chunks):
    rows = n_chunks * LC
    kq, kv, kg = B_HEADS, (2 * B_HEADS * B_DK) // B_DV, (2 * B_HEADS * B_DK + B_HEADS * B_DV) // B_DV
    masks = jnp.asarray(_gla_level_masks(), BF16)
    cmap = lambda b, h: (0, 0, 0)
    return pl.pallas_call(
        functools.partial(_gla_kernel, n_chunks=n_chunks),
        grid=(n_batch, B_HEADS),
        in_specs=[pl.BlockSpec((rows, B_DK), lambda b, h: (b, h)),
                  pl.BlockSpec((rows, B_DK), lambda b, h: (b, kq + h)),
                  pl.BlockSpec((rows, B_DV), lambda b, h: (b, kv + h)),
                  pl.BlockSpec((rows, B_DV), lambda b, h: (b, kg + h)),
                  pl.BlockSpec((rows, 128), lambda b, h: (b, 0)),
                  pl.BlockSpec((3, B_DK), lambda b, h: (0, h)),
                  pl.BlockSpec((3, B_DK), lambda b, h: (0, kq + h)),
                  pl.BlockSpec((3, B_DV), lambda b, h: (0, kv + h)),
                  pl.BlockSpec((1, B_DK), lambda b, h: (0, h)),
                  pl.BlockSpec((1, B_DK), lambda b, h: (0, kq + h)),
                  pl.BlockSpec((1, B_DV), lambda b, h: (0, kv + h)),
                  pl.BlockSpec((2, 128, B_DK), lambda b, h: (0, 0, h)),
                  pl.BlockSpec((2, B_DK), lambda b, h: (0, h)),
                  pl.BlockSpec((1, B_DV), lambda b, h: (0, h)),
                  pl.BlockSpec((2 * GLA_LEVELS, LC, LC), cmap)],
        out_specs=pl.BlockSpec((rows, B_DV), lambda b, h: (b, h)),
        out_shape=jax.ShapeDtypeStruct((n_batch * rows, B_HEADS * B_DV), BF16),
        scratch_shapes=[pltpu.VMEM((rows, B_DK), BF16), pltpu.VMEM((rows, B_DK), BF16),
                        pltpu.VMEM((rows, B_DV), BF16),
                        pltpu.VMEM((rows, B_DV), F32), pltpu.VMEM((rows, B_DV), F32),
                        pltpu.VMEM((2, B_DV, B_DK), F32)],
        compiler_params=_cparams("parallel", "parallel"),
        name="gla_scan",
    )(p, p, p, p, low, conv_w, conv_w, conv_w, conv_b, conv_b, conv_b, gate_w, gate_b, norm_g, masks)


FB = 256


def _dft_mats(n_len):
    n = 2 * n_len
    nfb = n_len // FB
    f = np.arange(n_len)[:, None]
    t = np.arange(n_len)[None, :]
    ang = 2.0 * np.pi * ((f * t) % n) / n
    alt = (-1.0) ** np.arange(n_len)
    cf, sf = np.cos(ang), -np.sin(ang)
    sf[0, :] = alt
    fwd = np.stack([cf.reshape(nfb, FB, n_len), sf.reshape(nfb, FB, n_len)], axis=1).reshape(n, n_len)
    ci, si = (2.0 / n) * np.cos(ang.T), -(2.0 / n) * np.sin(ang.T)
    ci[:, 0] = 1.0 / n
    si[:, 0] = alt / n
    inv = np.stack([ci.reshape(n_len, nfb, FB), si.reshape(n_len, nfb, FB)], axis=2).reshape(n_len, n)
    return fwd.astype(np.float32), inv.astype(np.float32)


def _hy_pre_kernel(x0_ref, x1_ref, v_ref, cw0, cw1, cw2, cb0, cb1, cb2, zc_ref, zl_ref, x0c_ref, x0l_ref,
                   *, n_rows):
    x0 = _conv3(x0_ref, cw0, cb0, n_rows)
    z = _conv3(x1_ref, cw1, cb1, n_rows) * _conv3(v_ref, cw2, cb2, n_rows)
    zc_ref[...] = z[:LC].astype(BF16)
    zl_ref[...] = z[LC:].astype(BF16)
    x0c_ref[...] = x0[:LC].astype(BF16)
    x0l_ref[...] = x0[LC:].astype(BF16)


def hyena_pre(p, conv_w, conv_b, n_batch, n_chunks):
    rows = n_chunks * LC
    d = p.shape[1] // 3
    w = 256
    nj = d // w
    s_len = rows - LC
    spec = lambda k: pl.BlockSpec((rows, w), lambda b, j: (b, k * nj + j))
    cws = lambda k: pl.BlockSpec((3, w), lambda b, j: (0, k * nj + j))
    cbs = lambda k: pl.BlockSpec((1, w), lambda b, j: (0, k * nj + j))
    oc = pl.BlockSpec((LC, w), lambda b, j: (b, j))
    ol = pl.BlockSpec((s_len, w), lambda b, j: (b, j))
    return pl.pallas_call(
        functools.partial(_hy_pre_kernel, n_rows=rows),
        grid=(n_batch, nj),
        in_specs=[spec(0), spec(1), spec(2), cws(0), cws(1), cws(2), cbs(0), cbs(1), cbs(2)],
        out_specs=[oc, ol, oc, ol],
        out_shape=[jax.ShapeDtypeStruct((n_batch * LC, d), BF16), jax.ShapeDtypeStruct((n_batch * s_len, d), BF16),
                   jax.ShapeDtypeStruct((n_batch * LC, d), BF16), jax.ShapeDtypeStruct((n_batch * s_len, d), BF16)],
        compiler_params=_cparams("parallel", "parallel"),
        name="hyena_pre",
    )(p, p, p, conv_w, conv_w, conv_w, conv_b, conv_b, conv_b)


def _filt_dft_kernel(f_ref, x_ref, o_ref):
    acc = _dot(f_ref[...], x_ref[0])
    half = acc.shape[1] // 2
    o_ref[0] = acc[:, :half] + acc[:, half:]


def filter_spectrum(fwd, hsum, hdiff):
    n_len, d = hsum.shape

    def hilo(a):
        hi = a.astype(BF16)
        return jnp.concatenate([hi, (a - hi.astype(F32)).astype(BF16)], axis=1)
    xs = jnp.stack([hilo(hsum), hilo(hdiff)])
    return pl.pallas_call(
        _filt_dft_kernel,
        grid=(2, n_len // FB),
        in_specs=[pl.BlockSpec((FB, n_len), lambda w, m: (2 * m + w, 0)),
                  pl.BlockSpec((1, n_len, 2 * d), lambda w, m: (w, 0, 0))],
        out_specs=pl.BlockSpec((1, FB, d), lambda w, m: (w, m, 0)),
        out_shape=jax.ShapeDtypeStruct((2, n_len, d), F32),
        compiler_params=_cparams("parallel", "parallel"),
        name="filter_spectrum",
    )(fwd, xs)


def _dft_fwd_kernel(f_ref, z_ref, kr_ref, ki_ref, y_ref):
    acc = _dot(f_ref[...], z_ref[...])
    zr, zi = acc[:FB], acc[FB:]
    kr, ki = kr_ref[...], ki_ref[...]
    packed = jnp.logical_and(pl.program_id(1) == 0, lax.broadcasted_iota(I32, (FB, 1), 0) == 0)
    y_ref[:FB] = (zr * kr - jnp.where(packed, 0.0, zi * ki)).astype(BF16)
    y_ref[FB:] = jnp.where(packed, zi * ki, zr * ki + zi * kr).astype(BF16)


def dft_forward(fwd, z, kr, ki, n_batch):
    n_len, d = kr.shape
    nfb = n_len // FB
    return pl.pallas_call(
        _dft_fwd_kernel,
        grid=(n_batch, nfb),
        in_specs=[pl.BlockSpec((2 * FB, n_len), lambda b, m: (m, 0)),
                  pl.BlockSpec((n_len, d), lambda b, m: (b, 0)),
                  pl.BlockSpec((FB, d), lambda b, m: (m, 0)),
                  pl.BlockSpec((FB, d), lambda b, m: (m, 0))],
        out_specs=pl.BlockSpec((2 * FB, d), lambda b, m: (b * nfb + m, 0)),
        out_shape=jax.ShapeDtypeStruct((n_batch * 2 * n_len, d), BF16),
        compiler_params=_cparams("parallel", "arbitrary"),
        name="dft_forward",
    )(fwd, z, kr, ki)


def _dft_inv_kernel(i_ref, y_ref, z_ref, x0_ref, skip_ref, *rest):
    o_ref = rest[-1]
    conv = _dot(i_ref[...], y_ref[...])
    o_ref[...] = (x0_ref[...].astype(F32) * (conv + z_ref[...].astype(F32) * skip_ref[...])).astype(o_ref.dtype)


def dft_inverse(inv, yf, z, x0, skip, n_batch, tiles_per_batch, first_tile, n_total_rows, prev=None):
    n_len, d = z.shape[0] // n_batch, z.shape[1]
    nt = n_len // TM
    in_specs = [pl.BlockSpec((TM, 2 * n_len), lambda b, m: (m, 0)),
                pl.BlockSpec((2 * n_len, d), lambda b, m: (b, 0)),
                pl.BlockSpec((TM, d), lambda b, m: (b * nt + m, 0)),
                pl.BlockSpec((TM, d), lambda b, m: (b * nt + m, 0)),
                pl.BlockSpec((1, d), lambda b, m: (0, 0))]
    args = [inv, yf, z, x0, skip.reshape(1, d)]
    aliases = {}
    if prev is not None:
        in_specs.append(pl.BlockSpec(memory_space=pl.ANY))
        args.append(prev)
        aliases = {5: 0}
    return pl.pallas_call(
        _dft_inv_kernel,
        grid=(n_batch, nt),
        in_specs=in_specs,
        out_specs=pl.BlockSpec((TM, d), lambda b, m: (b * tiles_per_batch + first_tile + m, 0)),
        out_shape=jax.ShapeDtypeStruct((n_total_rows, d), BF16),
        input_output_aliases=aliases,
        compiler_params=_cparams("parallel", "arbitrary"),
        name="dft_inverse",
    )(*args)


def _hyena_filters(n_len, w1, b1, w_mid, b_mid, w_out, freq):
    hp = lax.Precision.HIGHEST
    d = w_out.shape[1] // 2
    n_bands = (w1.shape[0] - 1) // 2
    t = jnp.linspace(0.0, 1.0, n_len, dtype=F32)[:, None]
    pos = jnp.arange(n_len, dtype=F32)[:, None]
    bands = jnp.linspace(1e-4, n_bands - 1, n_bands, dtype=F32)[None, :]
    ang = (2.0 * math.pi / n_len) * pos * bands
    feats = jnp.concatenate([t, jnp.cos(ang), -jnp.sin(ang)], -1)
    h = jnp.sin(freq * (jnp.dot(feats, w1, precision=hp) + b1))
    for m in range(w_mid.shape[0]):
        h = jnp.sin(freq * (jnp.dot(h, w_mid[m], precision=hp) + b_mid[m]))
    h = jnp.dot(h, w_out, precision=hp).reshape(n_len, 2, d)
    deltas = jnp.abs(jnp.linspace(math.log(1e-2) / 1.5, math.log(1e-2) / 0.3, d, dtype=F32))
    h = h * jnp.exp(-t * deltas)[:, None, :]
    hf, hb = h[:, 0], h[:, 1]
    l1 = jnp.sum(jnp.abs(hf), 0) + jnp.sum(jnp.abs(hb[1:]), 0)
    return hf / l1, hb / l1


def hyena_mix(p, conv_w, conv_b, filt, skip, n_batch, n_chunks):
    rows = n_chunks * LC
    zc, zl, x0c, x0l = hyena_pre(p, conv_w, conv_b, n_batch, n_chunks)
    y = None
    for z, x0, first_tile in ((zl, x0l, 1), (zc, x0c, 0)):
        n_len = z.shape[0] // n_batch
        fwd_np, inv_np = _dft_mats(n_len)
        fwd, inv = jnp.asarray(fwd_np, BF16), jnp.asarray(inv_np, BF16)
        hf, hb = _hyena_filters(n_len, *filt)
        hb0 = hb.at[0].set(0.0)
        hsum, hdiff = hf + hb0, hf - hb0
        spec = filter_spectrum(fwd, hsum, hdiff)
        alt = jnp.asarray((-1.0) ** np.arange(n_len), F32)[:, None]
        kr, ki = spec[0], spec[1].at[0].set(jnp.sum(alt * hsum, axis=0))
        yf = dft_forward(fwd, z, kr, ki, n_batch)
        y = dft_inverse(inv, yf, z, x0, skip, n_batch, n_chunks, first_tile, n_batch * rows, prev=y)
    return y


P_DTYPE = BF16
N_MAIN = 3072
N_EXTRA = 128


def _pad_cols(w, n):
    return jnp.pad(w, ((0, 0), (0, n - w.shape[1])))


def _to_scan_order(a, n_batch, ctx_len):
    w = a.shape[1]
    a3 = a.reshape(n_batch, -1, w)
    lat = a3[:, ctx_len:]
    s_len = lat.shape[1]
    lat = lat.reshape(n_batch, s_len // GRID_W, GRID_W, w).transpose(0, 2, 1, 3).reshape(n_batch, s_len, w)
    return jnp.concatenate([a3[:, :ctx_len], lat], axis=1).reshape(-1, w)


def _from_scan_order(a, n_batch, ctx_len):
    w = a.shape[1]
    a3 = a.reshape(n_batch, -1, w)
    lat = a3[:, ctx_len:]
    s_len = lat.shape[1]
    lat = lat.reshape(n_batch, GRID_W, s_len // GRID_W, w).transpose(0, 2, 1, 3).reshape(n_batch, s_len, w)
    return jnp.concatenate([a3[:, :ctx_len], lat], axis=1).reshape(-1, w)


def kernel(x, c, ctx, c_ctx, ada_w, ada_b, norm1_g, norm2_g, final_g, ml_w_in, ml_b_gate, ml_conv_w, ml_conv_b, ml_norm_g, ml_w_out, gla_w_in, gla_conv_w, gla_conv_b, gla_gate_w2, gla_gate_b, gla_norm_g, gla_w_out, hy_w_in, hy_b_in, hy_conv_w, hy_conv_b, hy_filt_w1, hy_filt_b1, hy_filt_w_mid, hy_filt_b_mid, hy_filt_w_out, hy_filt_freq, hy_skip, hy_w_out, hy_b_out, router_w, router_b, exp_w1, exp_w3, exp_w2):
    n_batch, s_len, d = x.shape
    ctx_len = ctx.shape[1]
    depth = ada_w.shape[0]
    assert ctx_len == LC == TM and s_len % LC == 0 and s_len % GRID_W == 0
    n_chunks = (ctx_len + s_len) // LC
    t_all = n_batch * n_chunks * LC
    lat_tiles = s_len // TM

    mod_rows = jnp.zeros((16, d), F32).at[0].set(c_ctx).at[1:1 + n_batch].set(c)
    mods = ada_mods(mod_rows, ada_w, ada_b).reshape(depth, 16, 6, d)
    xs = jnp.concatenate([ctx, x], axis=1).reshape(t_all, d)
    router_wt = router_w.T
    zeros_d = jnp.zeros((d,), F32)
    out = None

    for i in range(depth):
        last = i == depth - 1
        kind, j = i % 3, i // 3
        if kind == 0:
            w_in = _pad_cols(ml_w_in[j], N_MAIN + N_EXTRA).astype(BF16)
            bias = jnp.zeros((N_MAIN + N_EXTRA,), F32).at[N_MAIN:N_MAIN + 4 * A_HEADS].set(ml_b_gate[j])
            p, pg = in_proj(xs, mods[i], norm1_g[i], w_in, bias, N_MAIN, n_chunks, P_DTYPE)
            g4 = pg[:, :4 * A_HEADS].reshape(t_all, 4, A_HEADS // 2, 2)
            g_cols = g4.transpose(2, 0, 1, 3).reshape(A_HEADS // 2, t_all, 8)
            g_rows = g_cols.reshape(A_HEADS // 2, n_batch, n_chunks, LC, 8).transpose(0, 1, 2, 4, 3)
            y = mlstm_scan(p, g_cols, g_rows, ml_conv_w[j], ml_conv_b[j].reshape(1, -1),
                           ml_norm_g[j].reshape(1, -1), n_batch, n_chunks)
            w_out, b_out = ml_w_out[j], zeros_d
        elif kind == 1:
            w_in = _pad_cols(gla_w_in[j], N_MAIN + N_EXTRA).astype(BF16)
            bias = jnp.zeros((N_MAIN + N_EXTRA,), F32)
            p, pg = in_proj(xs, mods[i], norm1_g[i], w_in, bias, N_MAIN, n_chunks, P_DTYPE)
            p = _to_scan_order(p, n_batch, ctx_len)
            pg = _to_scan_order(pg, n_batch, ctx_len)
            gate_w = jnp.zeros((2, 128, B_HEADS * B_DK), F32)
            gate_w = gate_w.at[0, :B_RANK].set(gla_gate_w2[j, 0]).at[1, B_RANK:2 * B_RANK].set(gla_gate_w2[j, 1])
            y = gla_scan(p, pg, gla_conv_w[j], gla_conv_b[j].reshape(1, -1), gate_w, gla_gate_b[j],
                         gla_norm_g[j].reshape(1, -1), n_batch, n_chunks)
            y = _from_scan_order(y, n_batch, ctx_len)
            w_out, b_out = gla_w_out[j], zeros_d
        else:
            (p,) = in_proj(xs, mods[i], norm1_g[i], hy_w_in[j].astype(BF16), hy_b_in[j], N_MAIN, n_chunks, P_DTYPE)
            filt = (hy_filt_w1[j], hy_filt_b1[j], hy_filt_w_mid[j], hy_filt_b_mid[j], hy_filt_w_out[j],
                    hy_filt_freq[j])
            y = hyena_mix(p, hy_conv_w[j], hy_conv_b[j].reshape(1, -1), filt, hy_skip[j], n_batch, n_chunks)
            w_out, b_out = hy_w_out[j], hy_b_out[j]

        x2, h2, ids, gates, ranks, counts = out_route(
            xs, y, mods[i], w_out.astype(BF16), b_out, norm2_g[i], router_wt, router_b, n_chunks, 1, last)
        n_tok = h2.shape[0]
        dest, slot_tok, block_exp, n_used = moe_plan(ids, ranks, counts, n_tok)
        xb = jnp.take(h2, slot_tok, axis=0)
        yb = moe_experts(xb, block_exp, n_used, exp_w1, exp_w3, exp_w2, i)
        y0 = jnp.take(yb, dest[0], axis=0)
        y1 = jnp.take(yb, dest[1], axis=0)
        if last:
            out = moe_combine(x2, y0, y1, gates.T, mods[i], lambda t: 1 + t // lat_tiles, final_g)
        else:
            xs = moe_combine(x2, y0, y1, gates.T, mods[i],
                             lambda t: jnp.where(t % n_chunks == 0, 0, 1 + t // n_chunks))
    return out.reshape(n_batch, s_len, d)
```

```python
import functools
import math

import numpy as np
import jax
import jax.numpy as jnp
from jax import lax
from jax.experimental import pallas as pl
from jax.experimental.pallas import tpu as pltpu

F32 = jnp.float32
BF16 = jnp.bfloat16
I32 = jnp.int32

EPS = 1e-6
TM = 256
LC = 256
MOE_BLOCK = 512
N_EXPERTS = 16
N_GROUPS = 4
EPG = N_EXPERTS // N_GROUPS
A_HEADS, A_DK, A_DV = 8, 64, 128
B_HEADS, B_DK, B_DV = 4, 128, 256
B_RANK = 16
B_TAU = 16.0
GRID_W = 64
VMEM_LIMIT_V7X = 56 * 1024 * 1024


def _cparams(*sem):
    return pltpu.CompilerParams(dimension_semantics=sem, vmem_limit_bytes=VMEM_LIMIT_V7X)


def _silu(x):
    return x * (1.0 / (1.0 + jnp.exp(-x)))


def _sigmoid(x):
    return 1.0 / (1.0 + jnp.exp(-x))


def _log_sigmoid(x):
    return jnp.minimum(x, 0.0) - jnp.log(1.0 + jnp.exp(-jnp.abs(x)))


def _split3(x):
    hi = x.astype(BF16)
    r1 = x - hi.astype(F32)
    mid = r1.astype(BF16)
    lo = (r1 - mid.astype(F32)).astype(BF16)
    return hi, mid, lo


def _dot(a, b):
    return jnp.dot(a, b, preferred_element_type=F32)


def _dot_nt(a, b):
    return lax.dot_general(a, b, (((1,), (1,)), ((), ())), preferred_element_type=F32)


def _dot_tn(a, b):
    return lax.dot_general(a, b, (((0,), (0,)), ((), ())), preferred_element_type=F32)


LANES = 128
SUBLANES = 8


def _tok_tiles_store(ref, base, x):
    n, d = x.shape
    per = d // LANES
    for c in range(per):
        ref[pl.ds(base + c, n, stride=per), :] = x[:, c * LANES:(c + 1) * LANES]


def _tok_tiles_load(ref, base, n, d):
    per = d // LANES
    return jnp.concatenate([ref[pl.ds(base + c, n, stride=per), :] for c in range(per)], axis=1)


def _dot_exact_lhs01(m01, x):
    hi, mid, lo = _split3(x)
    return (_dot(m01, hi) + _dot(m01, mid)) + _dot(m01, lo)


def _dot_exact_rhs01(x, m01):
    hi, mid, lo = _split3(x)
    return (_dot(hi, m01) + _dot(mid, m01)) + _dot(lo, m01)


def _ada_kernel(s_ref, w_ref, b_ref, o_ref):
    s = _silu(s_ref[...]).astype(BF16)
    o_ref[0] = _dot(s, w_ref[0].astype(BF16)) + b_ref[0]


def ada_mods(rows, ada_w, ada_b):
    depth, d, n6 = ada_w.shape
    nr = rows.shape[0]
    tn = 1536
    return pl.pallas_call(
        _ada_kernel,
        grid=(depth, n6 // tn),
        in_specs=[pl.BlockSpec((nr, d), lambda l, j: (0, 0)),
                  pl.BlockSpec((1, d, tn), lambda l, j: (l, 0, j)),
                  pl.BlockSpec((1, 1, tn), lambda l, j: (l, 0, j))],
        out_specs=pl.BlockSpec((1, nr, tn), lambda l, j: (l, 0, j)),
        out_shape=jax.ShapeDtypeStruct((depth, nr, n6), F32),
        compiler_params=_cparams("parallel", "parallel"),
        name="ada_mods",
    )(rows, ada_w, ada_b.reshape(depth, 1, n6))


def _norm_mod(x, g, shift, scale):
    y = x * lax.rsqrt(jnp.mean(x * x, axis=-1, keepdims=True) + EPS) * g
    return y * (1.0 + scale) + shift


def _in_proj_kernel(x_ref, mod_ref, g_ref, w_ref, b_ref, p_ref, *rest, n_main):
    h = _norm_mod(x_ref[...], g_ref[...], mod_ref[0, 0:1, :], mod_ref[0, 1:2, :]).astype(BF16)
    p_ref[...] = (_dot(h, w_ref[:, :n_main]) + b_ref[:, :n_main]).astype(p_ref.dtype)
    if rest:
        rest[0][...] = _dot(h, w_ref[:, n_main:]) + b_ref[:, n_main:]


def in_proj(x, mods_l, g, w, b, n_main, tiles_per_batch, p_dtype):
    t, d = x.shape
    n = w.shape[1]
    n_extra = n - n_main
    mod_row = lambda i: jnp.where(i % tiles_per_batch == 0, 0, 1 + i // tiles_per_batch)
    out_shape = [jax.ShapeDtypeStruct((t, n_main), p_dtype)]
    out_specs = [pl.BlockSpec((TM, n_main), lambda i: (i, 0))]
    if n_extra:
        out_shape.append(jax.ShapeDtypeStruct((t, n_extra), F32))
        out_specs.append(pl.BlockSpec((TM, n_extra), lambda i: (i, 0)))
    return pl.pallas_call(
        functools.partial(_in_proj_kernel, n_main=n_main),
        grid=(t // TM,),
        in_specs=[pl.BlockSpec((TM, d), lambda i: (i, 0)),
                  pl.BlockSpec((1, 6, d), lambda i: (mod_row(i), 0, 0)),
                  pl.BlockSpec((1, d), lambda i: (0, 0)),
                  pl.BlockSpec((d, n), lambda i: (0, 0)),
                  pl.BlockSpec((1, n), lambda i: (0, 0))],
        out_specs=out_specs,
        out_shape=out_shape,
        compiler_params=_cparams("parallel"),
        name="in_proj",
    )(x, mods_l, g.reshape(1, d), w, b.reshape(1, n))


def _first_argmax(vals):
    best, idx = vals[0], jnp.zeros(vals[0].shape, I32)
    for i in range(1, len(vals)):
        take = vals[i] > best
        best = jnp.where(take, vals[i], best)
        idx = jnp.where(take, i, idx)
    return best, idx


def _select_row(rows, idx):
    out = rows[0]
    for i in range(1, len(rows)):
        out = jnp.where(idx == i, rows[i], out)
    return out


def _route(scores, sel):
    neg = jnp.float32(-jnp.inf)
    srow = [sel[e:e + 1, :] for e in range(N_EXPERTS)]
    crow = [scores[e:e + 1, :] for e in range(N_EXPERTS)]
    gscore = []
    for g in range(N_GROUPS):
        a = srow[g * EPG:(g + 1) * EPG]
        m1, i1 = _first_argmax(a)
        m2, _ = _first_argmax([jnp.where(i1 == i, neg, a[i]) for i in range(EPG)])
        gscore.append(m1 + m2)
    _, grp = _first_argmax(gscore)
    in_sel = [_select_row([srow[g * EPG + i] for g in range(N_GROUPS)], grp) for i in range(EPG)]
    in_sc = [_select_row([crow[g * EPG + i] for g in range(N_GROUPS)], grp) for i in range(EPG)]
    _, l1 = _first_argmax(in_sel)
    _, l2 = _first_argmax([jnp.where(l1 == i, neg, in_sel[i]) for i in range(EPG)])
    g1 = _select_row(in_sc, l1)
    g2 = _select_row(in_sc, l2)
    tot = g1 + g2
    ids = jnp.concatenate([grp * EPG + l1, grp * EPG + l2], axis=0)
    gates = jnp.concatenate([g1 / tot, g2 / tot], axis=0)
    return ids, gates


def _out_route_kernel(xa_ref, xb_ref, ya_ref, yb_ref, mod_ref, w_ref, b_ref, g_ref, rw_ref, rb_ref,
                      xo_ref, h_ref, ids_ref, gates_ref, rank_ref, cnt_ref, carry_ref, *, mod_row):
    i = pl.program_id(0)

    @pl.when(i == 0)
    def _():
        carry_ref[...] = jnp.zeros_like(carry_ref)

    halves = (0, 1)
    rows = [slice(k * TM, (k + 1) * TM) for k in halves]
    mod = [mod_ref[mod_row(2 * i + k)] for k in halves]
    proj = [_dot(y_ref[...], w_ref[...]) for y_ref in (ya_ref, yb_ref)]
    xn = [x_ref[...] + mod[k][2:3, :] * (proj[k] + b_ref[...]) for k, x_ref in enumerate((xa_ref, xb_ref))]
    h = [_norm_mod(xn[k], g_ref[...], mod[k][3:4, :], mod[k][4:5, :]) for k in halves]
    logits = [lax.dot_general(rw_ref[...], h[k], (((1,), (1,)), ((), ())),
                              precision=lax.Precision.HIGHEST, preferred_element_type=F32) for k in halves]
    for k in halves:
        xo_ref[rows[k], :] = xn[k]
        _tok_tiles_store(h_ref, k * TM * SUBLANES, h[k])
    scores = [_sigmoid(logits[k]) for k in halves]
    routes = [_route(scores[k], scores[k] + rb_ref[:, 0:1]) for k in halves]
    erow = lax.broadcasted_iota(I32, (N_EXPERTS, TM), 0)
    before = (lax.broadcasted_iota(I32, (TM, TM), 0) < lax.broadcasted_iota(I32, (TM, TM), 1)).astype(BF16)
    oh = [[(erow == routes[k][0][j:j + 1, :]).astype(F32) for j in range(2)] for k in halves]
    pre = [[_dot(oh[k][j].astype(BF16), before) for j in range(2)] for k in halves]
    tot = [[jnp.sum(oh[k][j], axis=1, keepdims=True) for j in range(2)] for k in halves]

    carry = carry_ref[:, 0:1]
    for k in halves:
        ids_ref[:, rows[k]] = routes[k][0]
        gates_ref[:, rows[k]] = routes[k][1]
        r0 = jnp.sum(oh[k][0] * (carry + pre[k][0]), axis=0, keepdims=True)
        r1 = jnp.sum(oh[k][1] * (carry + tot[k][0] + pre[k][1]), axis=0, keepdims=True)
        rank_ref[:, rows[k]] = jnp.concatenate([r0, r1], axis=0).astype(I32)
        carry = carry + tot[k][0] + tot[k][1]
    carry_ref[...] = jnp.broadcast_to(carry, carry_ref.shape)
    cnt_ref[...] = jnp.broadcast_to(carry, cnt_ref.shape).astype(I32)


def out_route(x, y, mods_l, w_out, b_out, g2n, router_wt, router_b, tiles_per_batch, ctx_tiles, skip_ctx):
    t, d = x.shape
    nb = t // (tiles_per_batch * TM)
    if skip_ctx:
        per = tiles_per_batch - ctx_tiles
        src = lambda i: (i // per) * tiles_per_batch + ctx_tiles + i % per
        mod_row = lambda i: 1 + i // per
        n_tiles = nb * per
    else:
        src = lambda i: i
        mod_row = lambda i: jnp.where(i % tiles_per_batch < ctx_tiles, 0, 1 + i // tiles_per_batch)
        n_tiles = nb * tiles_per_batch
    assert n_tiles % 2 == 0
    tr = n_tiles * TM
    tok = lambda i: (i, 0)
    row2 = lambda i: (0, i)
    outs = pl.pallas_call(
        functools.partial(_out_route_kernel, mod_row=mod_row),
        grid=(n_tiles // 2,),
        in_specs=[pl.BlockSpec((TM, d), lambda i: (src(2 * i), 0)),
                  pl.BlockSpec((TM, d), lambda i: (src(2 * i + 1), 0)),
                  pl.BlockSpec((TM, d), lambda i: (src(2 * i), 0)),
                  pl.BlockSpec((TM, d), lambda i: (src(2 * i + 1), 0)),
                  pl.BlockSpec(mods_l.shape, lambda i: (0, 0, 0)),
                  pl.BlockSpec((d, d), lambda i: (0, 0)),
                  pl.BlockSpec((1, d), lambda i: (0, 0)),
                  pl.BlockSpec((1, d), lambda i: (0, 0)),
                  pl.BlockSpec((N_EXPERTS, d), lambda i: (0, 0)),
                  pl.BlockSpec((N_EXPERTS, 128), lambda i: (0, 0))],
        out_specs=[pl.BlockSpec((2 * TM, d), tok),
                   pl.BlockSpec((2 * TM * SUBLANES, LANES), tok),
                   pl.BlockSpec((2, 2 * TM), row2),
                   pl.BlockSpec((2, 2 * TM), row2),
                   pl.BlockSpec((2, 2 * TM), row2),
                   pl.BlockSpec((N_EXPERTS, 128), lambda i: (0, 0))],
        out_shape=[jax.ShapeDtypeStruct((tr, d), F32),
                   jax.ShapeDtypeStruct((tr * SUBLANES, LANES), F32),
                   jax.ShapeDtypeStruct((2, tr), I32),
                   jax.ShapeDtypeStruct((2, tr), F32),
                   jax.ShapeDtypeStruct((2, tr), I32),
                   jax.ShapeDtypeStruct((N_EXPERTS, 128), I32)],
        scratch_shapes=[pltpu.VMEM((N_EXPERTS, 128), F32)],
        compiler_params=_cparams("arbitrary"),
        name="out_route",
    )(x, x, y, y, mods_l, w_out, b_out.reshape(1, d), g2n.reshape(1, d), router_wt,
      jnp.broadcast_to(router_b.reshape(N_EXPERTS, 1), (N_EXPERTS, 128)))
    return outs


def _moe_kernel(bexp_ref, nused_ref, x_ref, w1_ref, w3_ref, w2_ref, o_ref, w1b, w3b, w2b):
    i = pl.program_id(0)
    prev = bexp_ref[jnp.maximum(i - 1, 0)]
    fresh = jnp.logical_or(i == 0, bexp_ref[i] != prev)
    used = i < nused_ref[0]

    @pl.when(jnp.logical_and(fresh, used))
    def _():
        w1b[...] = w1_ref[0].astype(BF16)
        w3b[...] = w3_ref[0].astype(BF16)
        w2b[...] = w2_ref[0].astype(BF16)

    @pl.when(used)
    def _():
        x = _tok_tiles_load(x_ref, 0, MOE_BLOCK, w1b.shape[0]).astype(BF16)
        a = _dot(x, w1b[...])
        b = _dot(x, w3b[...])
        hmid = (_silu(a) * b).astype(BF16)
        _tok_tiles_store(o_ref, 0, _dot(hmid, w2b[...]))

    @pl.when(jnp.logical_not(used))
    def _():
        o_ref[...] = jnp.zeros_like(o_ref)


def moe_experts(xb, block_exp, n_used, w1, w3, w2, layer):
    d, ff = w1.shape[2], w1.shape[3]
    per = d // LANES
    n_slots, dh = xb.shape[0] // per, LANES
    n_blocks = n_slots // MOE_BLOCK
    wmap = lambda i, be, nu: (layer, be[i], 0, 0)
    return pl.pallas_call(
        _moe_kernel,
        grid_spec=pltpu.PrefetchScalarGridSpec(
            num_scalar_prefetch=2,
            grid=(n_blocks,),
            in_specs=[pl.BlockSpec((MOE_BLOCK * per, dh), lambda i, be, nu: (i, 0)),
                      pl.BlockSpec((None, 1, d, ff), wmap),
                      pl.BlockSpec((None, 1, d, ff), wmap),
                      pl.BlockSpec((None, 1, ff, d), wmap)],
            out_specs=pl.BlockSpec((MOE_BLOCK * per, dh), lambda i, be, nu: (i, 0)),
            scratch_shapes=[pltpu.VMEM((d, ff), BF16), pltpu.VMEM((d, ff), BF16),
                            pltpu.VMEM((ff, d), BF16)]),
        out_shape=jax.ShapeDtypeStruct((n_slots * per, dh), F32),
        compiler_params=_cparams("arbitrary"),
        name="moe_experts",
    )(block_exp, n_used, xb, w1, w3, w2)


def moe_plan(ids, ranks, counts, n_tokens):
    n_assign = 2 * n_tokens
    n_blocks = n_assign // MOE_BLOCK + N_EXPERTS
    cnt = counts[:, 0]
    padded = (cnt + MOE_BLOCK - 1) // MOE_BLOCK * MOE_BLOCK
    pad_end = jnp.cumsum(padded)
    pad_start = pad_end - padded
    start_of = sum(jnp.where(ids == e, pad_start[e], 0) for e in range(N_EXPERTS))
    dest = start_of + ranks
    blk_start = jnp.arange(n_blocks, dtype=I32) * MOE_BLOCK
    block_exp = jnp.minimum(jnp.sum(blk_start[:, None] >= pad_end[None, :], axis=1), N_EXPERTS - 1).astype(I32)
    n_used = (pad_end[-1] // MOE_BLOCK).astype(I32).reshape(1)
    dest_tiles = dest.reshape(2, n_tokens // TM, TM).transpose(1, 0, 2)
    return dest_tiles, n_blocks * MOE_BLOCK, block_exp, n_used


def _tile_of(idx):
    return pl.ds(pl.multiple_of(idx * SUBLANES, SUBLANES), SUBLANES)


def _token_copies(src_of, dst_of, dest_ref, sem):
    def body(r, carry):
        for k in range(2):
            pltpu.make_async_copy(src_of(k, r, dest_ref[0, k, r]), dst_of(k, r, dest_ref[0, k, r]), sem).start()
        return carry
    lax.fori_loop(0, TM, body, 0, unroll=8)


def _dispatch_kernel(dest_ref, h_ref, xb_in_ref, xb_ref, sem):
    del xb_in_ref
    _token_copies(lambda k, r, s: h_ref.at[_tile_of(r)], lambda k, r, s: xb_ref.at[_tile_of(s)], dest_ref, sem)
    for _ in range(2):
        pltpu.make_async_copy(h_ref, xb_ref.at[pl.ds(0, TM * SUBLANES)], sem).wait()


def moe_dispatch(h, dest_tiles, n_slots):
    rows, dh = h.shape
    return pl.pallas_call(
        _dispatch_kernel,
        grid=(rows // (TM * SUBLANES),),
        in_specs=[pl.BlockSpec((1, 2, TM), lambda i: (i, 0, 0), memory_space=pltpu.SMEM),
                  pl.BlockSpec((TM * SUBLANES, dh), lambda i: (i, 0)),
                  pl.BlockSpec(memory_space=pl.ANY)],
        out_specs=pl.BlockSpec(memory_space=pl.ANY),
        out_shape=jax.ShapeDtypeStruct((n_slots * SUBLANES, dh), h.dtype),
        scratch_shapes=[pltpu.SemaphoreType.DMA(())],
        input_output_aliases={2: 0},
        compiler_params=_cparams("arbitrary"),
        name="moe_dispatch",
    )(dest_tiles, h, jnp.zeros((n_slots * SUBLANES, dh), h.dtype))


def _combine_kernel(dcur_ref, dnext_ref, x_ref, yb_ref, gt_ref, mod_ref, *rest, final):
    buf, sem = rest[-2:]
    i = pl.program_id(0)
    n = pl.num_programs(0)

    def fetch(dest_ref, slot):
        _token_copies(lambda k, r, s: yb_ref.at[_tile_of(s)], lambda k, r, s: buf.at[slot, k, _tile_of(r)],
                      dest_ref, sem.at[slot])

    @pl.when(i == 0)
    def _():
        fetch(dcur_ref, 0)

    @pl.when(i + 1 < n)
    def _():
        fetch(dnext_ref, (i + 1) % 2)

    slot = i % 2
    for k in range(2):
        pltpu.make_async_copy(yb_ref.at[pl.ds(0, TM * SUBLANES)], buf.at[slot, k], sem.at[slot]).wait()
    d = x_ref.shape[1]
    moe = (gt_ref[:, 0:1] * _tok_tiles_load(buf.at[slot, 0], 0, TM, d)
           + gt_ref[:, 1:2] * _tok_tiles_load(buf.at[slot, 1], 0, TM, d))
    xn = x_ref[...] + mod_ref[0, 5:6, :] * moe
    if final:
        g_ref, o_ref = rest[:2]
        o_ref[...] = xn * lax.rsqrt(jnp.mean(xn * xn, axis=-1, keepdims=True) + EPS) * g_ref[...]
    else:
        rest[0][...] = xn


def moe_combine(x, yb, dest_tiles, gates_t, mods_l, mod_row, final_g=None):
    t, d = x.shape
    n_tiles = t // TM
    tok = lambda i: (i, 0)
    in_specs = [pl.BlockSpec((1, 2, TM), lambda i: (i, 0, 0), memory_space=pltpu.SMEM),
                pl.BlockSpec((1, 2, TM), lambda i: (jnp.minimum(i + 1, n_tiles - 1), 0, 0), memory_space=pltpu.SMEM),
                pl.BlockSpec((TM, d), tok),
                pl.BlockSpec(memory_space=pl.ANY),
                pl.BlockSpec((TM, 2), tok),
                pl.BlockSpec((1, 6, d), lambda i: (mod_row(i), 0, 0))]
    args = [dest_tiles, dest_tiles, x, yb, gates_t, mods_l]
    if final_g is not None:
        in_specs.append(pl.BlockSpec((1, d), lambda i: (0, 0)))
        args.append(final_g.reshape(1, d))
    return pl.pallas_call(
        functools.partial(_combine_kernel, final=final_g is not None),
        grid=(n_tiles,),
        in_specs=in_specs,
        out_specs=pl.BlockSpec((TM, d), tok),
        out_shape=jax.ShapeDtypeStruct((t, d), F32),
        scratch_shapes=[pltpu.VMEM((2, 2, TM * SUBLANES, LANES), F32), pltpu.SemaphoreType.DMA((2,))],
        compiler_params=_cparams("arbitrary"),
        name="moe_combine",
    )(*args)


def _conv3(x_ref, cw_ref, cb_ref, n_rows):
    x = x_ref[...].astype(F32)
    row = lax.broadcasted_iota(I32, (n_rows, 1), 0)
    first = jnp.logical_or(row == 0, row == LC)
    last = jnp.logical_or(row == LC - 1, row == n_rows - 1)
    prev = jnp.where(first, 0.0, pltpu.roll(x, 1, 0))
    nxt = jnp.where(last, 0.0, pltpu.roll(x, n_rows - 1, 0))
    return prev * cw_ref[0:1, :] + x * cw_ref[1:2, :] + nxt * cw_ref[2:3, :] + cb_ref[...]


def _conv_silu(x_ref, cw_ref, cb_ref, n_rows):
    return _silu(_conv3(x_ref, cw_ref, cb_ref, n_rows))


def _tri_masks():
    ri = lax.broadcasted_iota(I32, (LC, LC), 0)
    ci = lax.broadcasted_iota(I32, (LC, LC), 1)
    return ri >= ci, ri <= ci


def _scan_schedule(n_chunks, step):
    step(0, 0)

    def body(t, carry):
        step(t, n_chunks - t)
        return carry
    lax.fori_loop(1, n_chunks, body, 0)


def _cummax_lanes(x, fwd):
    n = x.shape[1]
    lane = lax.broadcasted_iota(I32, x.shape, 1)
    neg = jnp.float32(-jnp.inf)
    k = 1
    while k < n:
        if fwd:
            shifted = jnp.where(lane >= k, pltpu.roll(x, k, 1), neg)
        else:
            shifted = jnp.where(lane < n - k, pltpu.roll(x, n - k, 1), neg)
        x = jnp.maximum(x, shifted)
        k *= 2
    return x


def _rows16(rows):
    n = rows[0].shape[1]
    ri = lax.broadcasted_iota(I32, (16, n), 0)
    out = jnp.zeros((16, n), F32)
    for i, r in enumerate(rows):
        out = jnp.where(ri == i, jnp.broadcast_to(r, (16, n)), out)
    return out.astype(BF16)


def _mlstm_kernel(q_ref, k_ref, v_ref, o_ref, gt_ref, cwq_ref, cwk_ref, cbq_ref, cbk_ref, ng_ref,
                  y_ref, qs, ks, kst, vt, yft, ybt, ct_scr, m_scr, *, n_chunks):
    n_rows = n_chunks * LC
    qs[...] = (_conv_silu(q_ref, cwq_ref, cbq_ref, n_rows) * (A_DK ** -0.5)).astype(BF16)
    kf = _conv_silu(k_ref, cwk_ref, cbk_ref, n_rows)
    ks[...] = kf.astype(BF16)
    for c in range(n_chunks):
        kst[c] = kf[c * LC:(c + 1) * LC, :].T.astype(BF16)
        vt[c] = v_ref[c * LC:(c + 1) * LC, :].astype(F32).T.astype(BF16)
    ct_scr[...] = jnp.zeros_like(ct_scr)
    m_scr[...] = jnp.zeros_like(m_scr)
    tril, triu = _tri_masks()
    tril_b, triu_b = tril.astype(BF16), triu.astype(BF16)
    ones_rows = (lax.broadcasted_iota(I32, (A_DV, LC), 0) == 0).astype(BF16)
    one = jnp.ones((1, LC), F32)
    neg = jnp.float32(-jnp.inf)

    f32 = lambda t: t.astype(F32)

    def step(cf, cb):
        chains = [(cf, True, 0), (cf, True, 1), (cb, False, 0), (cb, False, 1)]
        m_prev = [m_scr[i, 0:1, 0:1] for i in range(4)]
        ct = [ct_scr[i] for i in range(4)]
        gts = {True: gt_ref[0, 0, cf], False: gt_ref[0, 0, cb]}
        css = {True: _dot_exact_rhs01(_log_sigmoid(gts[True]), triu_b),
               False: _dot_exact_rhs01(_log_sigmoid(gts[False]), tril_b)}
        rows = [pl.ds(pl.multiple_of(c * LC, LC), LC) for c, _, _ in chains]
        qh = [qs[rows[i], hh * A_DK:(hh + 1) * A_DK] for i, (_, _, hh) in enumerate(chains)]
        kh = [ks[rows[i], hh * A_DK:(hh + 1) * A_DK] for i, (_, _, hh) in enumerate(chains)]
        qk_t = [_dot_nt(kh[i], qh[i]) for i in range(4)]
        fcum, g, log_d = [], [], []
        for c, fwd, hh in chains:
            d = 0 if fwd else 1
            fc_row = css[fwd][4 * d + 2 + hh:4 * d + 3 + hh, :]
            g_row = gts[fwd][4 * d + hh:4 * d + hh + 1, :] - fc_row
            f_hi, f_mid, f_lo = _split3(fc_row)
            g_hi, g_mid, g_lo = _split3(g_row)
            f_slab = _rows16([f32(f_hi), f32(f_mid), f32(f_lo), one, one, one])
            g_slab = _rows16([one, one, one, f32(g_hi), f32(g_mid), f32(g_lo)])
            log_d.append(jnp.where(triu if fwd else tril, _dot_tn(g_slab, f_slab), neg))
            fcum.append(fc_row)
            g.append(g_row)
        li = [fcum[i] + m_prev[i] for i in range(4)]
        m_row = [jnp.maximum(li[i], jnp.max(log_d[i], axis=0, keepdims=True)) for i in range(4)]
        sm_t = [(qk_t[i] * jnp.exp(log_d[i] - m_row[i])).astype(BF16) for i in range(4)]
        vext_t = [jnp.concatenate([vt[c, hh * A_DV:(hh + 1) * A_DV, :], ones_rows], axis=0)
                  for c, _, hh in chains]
        intra = [_dot(vext_t[i], sm_t[i]) for i in range(4)]
        inter = [_dot_nt(ct[i].astype(BF16), qh[i]) for i in range(4)]
        kw_t, m_new = [], []
        for i, (c, fwd, hh) in enumerate(chains):
            num_t = intra[i] + jnp.exp(li[i] - m_row[i]) * inter[i]
            den = num_t[A_DV:A_DV + 1, :]
            (yft if fwd else ybt)[c, hh * A_DV:(hh + 1) * A_DV, :] = num_t[:A_DV, :] * (
                1.0 / jnp.maximum(jnp.abs(den), jnp.exp(-m_row[i])))
            b_end = fcum[i][:, LC - 1:LC] if fwd else fcum[i][:, 0:1]
            m_new.append(jnp.maximum(b_end + m_prev[i], jnp.max(b_end + g[i], axis=1, keepdims=True)))
            kw_t.append((kst[c, hh * A_DK:(hh + 1) * A_DK, :].astype(F32)
                         * jnp.exp(b_end + g[i] - m_new[i])).astype(BF16))
            ct[i] = jnp.exp(b_end + m_prev[i] - m_new[i]) * ct[i]
        upd = [_dot_nt(vext_t[i], kw_t[i]) for i in range(4)]
        for i in range(4):
            ct_scr[i] = ct[i] + upd[i]
            m_scr[i] = jnp.broadcast_to(m_new[i], m_scr.shape[1:])
    _scan_schedule(n_chunks, step)

    for hh in range(2):
        sl = slice(hh * A_DV, (hh + 1) * A_DV)
        for c in range(n_chunks):
            y_t = yft[c, sl, :] + ybt[c, sl, :]
            yn = (y_t * lax.rsqrt(jnp.mean(y_t * y_t, axis=0, keepdims=True) + EPS)).T
            rs = slice(c * LC, (c + 1) * LC)
            y_ref[rs, sl] = (yn * ng_ref[:, sl] * _sigmoid(o_ref[rs, sl].astype(F32))).astype(y_ref.dtype)


def mlstm_scan(p, g_rows, conv_w, conv_b, norm_g, n_batch, n_chunks):
    rows = n_chunks * LC
    n_hp = A_HEADS // 2
    qk_w = 2 * A_DK
    v_w = 2 * A_DV
    kq, kv, ko = (A_HEADS * A_DK) // qk_w, (2 * A_HEADS * A_DK) // v_w, (2 * A_HEADS * A_DK + A_HEADS * A_DV) // v_w
    return pl.pallas_call(
        functools.partial(_mlstm_kernel, n_chunks=n_chunks),
        grid=(n_batch, n_hp),
        in_specs=[pl.BlockSpec((rows, qk_w), lambda b, h: (b, h)),
                  pl.BlockSpec((rows, qk_w), lambda b, h: (b, kq + h)),
                  pl.BlockSpec((rows, v_w), lambda b, h: (b, kv + h)),
                  pl.BlockSpec((rows, v_w), lambda b, h: (b, ko + h)),
                  pl.BlockSpec((1, 1, n_chunks, 8, LC), lambda b, h: (h, b, 0, 0, 0)),
                  pl.BlockSpec((3, qk_w), lambda b, h: (0, h)),
                  pl.BlockSpec((3, qk_w), lambda b, h: (0, kq + h)),
                  pl.BlockSpec((1, qk_w), lambda b, h: (0, h)),
                  pl.BlockSpec((1, qk_w), lambda b, h: (0, kq + h)),
                  pl.BlockSpec((1, v_w), lambda b, h: (0, h))],
        out_specs=pl.BlockSpec((rows, v_w), lambda b, h: (b, h)),
        out_shape=jax.ShapeDtypeStruct((n_batch * rows, A_HEADS * A_DV), BF16),
        scratch_shapes=[pltpu.VMEM((rows, qk_w), BF16), pltpu.VMEM((rows, qk_w), BF16),
                        pltpu.VMEM((n_chunks, qk_w, LC), BF16), pltpu.VMEM((n_chunks, v_w, LC), BF16),
                        pltpu.VMEM((n_chunks, v_w, LC), F32), pltpu.VMEM((n_chunks, v_w, LC), F32),
                        pltpu.VMEM((4, 2 * A_DV, A_DK), F32), pltpu.VMEM((4, 8, 128), F32)],
        compiler_params=_cparams("parallel", "parallel"),
        name="mlstm_scan",
    )(p, p, p, p, g_rows, conv_w, conv_w, conv_b, conv_b, norm_g)


GLA_LEVELS = int(math.log2(LC))


def _gla_level_masks():
    j = np.arange(LC)[:, None]
    s = np.arange(LC)[None, :]
    out = []
    for fwd in (True, False):
        for lev in range(GLA_LEVELS):
            m = 1 << lev
            same = (j // (2 * m)) == (s // (2 * m))
            hi_j, hi_s = (j % (2 * m)) >= m, (s % (2 * m)) >= m
            out.append(same & (hi_j & ~hi_s if fwd else ~hi_j & hi_s))
    return np.stack(out).astype(np.float32)


def _seg_ref(b, m, fwd):
    n, w = b.shape
    r = m - 1 if fwd else m
    if 2 * m >= 8:
        b3 = b.reshape(n // (2 * m), 2 * m, w)
        return jnp.broadcast_to(b3[:, r:r + 1, :], b3.shape).reshape(n, w)
    b3 = b.reshape(n // 8, 8, w)
    sub = lax.broadcasted_iota(I32, b3.shape, 1)
    out = None
    for blk in range(8 // (2 * m)):
        row = jnp.broadcast_to(b3[:, blk * 2 * m + r:blk * 2 * m + r + 1, :], b3.shape)
        out = row if out is None else jnp.where(sub >= blk * 2 * m, row, out)
    return out.reshape(n, w)


def _gla_kernel(q_ref, k_ref, v_ref, g_ref, low_ref, cwq_ref, cwk_ref, cwv_ref, cbq_ref, cbk_ref, cbv_ref,
                gw_ref, gb_ref, ng_ref, mask_ref, y_ref, qs, ks, vs, yf, yb, st_scr, *, n_chunks):
    n_rows = n_chunks * LC
    qs[...] = (_conv_silu(q_ref, cwq_ref, cbq_ref, n_rows) * (B_DK ** -0.5)).astype(BF16)
    ks[...] = _conv_silu(k_ref, cwk_ref, cbk_ref, n_rows).astype(BF16)
    vs[...] = _conv_silu(v_ref, cwv_ref, cbv_ref, n_rows).astype(BF16)
    st_scr[...] = jnp.zeros_like(st_scr)
    tril, triu = _tri_masks()
    tril_b, triu_b = tril.astype(BF16), triu.astype(BF16)
    eye = jnp.logical_and(tril, triu).astype(F32)

    def step(cf, cb):
        dirs = (0, 1)
        fwd = (True, False)
        rows = [pl.ds(pl.multiple_of(c * LC, LC), LC) for c in (cf, cb)]
        st = [st_scr[d] for d in dirs]
        pre = [jnp.dot(low_ref[rows[d], :], gw_ref[d], precision=lax.Precision.HIGHEST,
                       preferred_element_type=F32) + gb_ref[d:d + 1, :] for d in dirs]
        lg = [_log_sigmoid(pre[d]) * (1.0 / B_TAU) for d in dirs]
        b = [_dot_exact_lhs01(tril_b if fwd[d] else triu_b, lg[d]) for d in dirs]
        q = [qs[rows[d], :].astype(F32) for d in dirs]
        k = [ks[rows[d], :].astype(F32) for d in dirs]
        v = [vs[rows[d], :] for d in dirs]
        a = [eye * jnp.sum(q[d] * k[d], axis=-1, keepdims=True) for d in dirs]
        for lev in range(GLA_LEVELS):
            for d in dirs:
                bref = _seg_ref(b[d], 1 << lev, fwd[d])
                qt = (q[d] * jnp.exp(jnp.minimum(b[d] - bref, 0.0))).astype(BF16)
                kt = (k[d] * jnp.exp(jnp.minimum(bref - b[d], 0.0))).astype(BF16)
                a[d] = a[d] + mask_ref[d * GLA_LEVELS + lev] * _dot_nt(qt, kt)
        intra = [_dot(a[d].astype(BF16), v[d]) for d in dirs]
        inter = [_dot_nt((q[d] * jnp.exp(b[d])).astype(BF16), st[d].astype(BF16)) for d in dirs]
        yf[rows[0], :] = intra[0] + inter[0]
        yb[rows[1], :] = intra[1] + inter[1]
        b_end = [b[0][LC - 1:LC, :], b[1][0:1, :]]
        upd = [_dot_tn(v[d], (k[d] * jnp.exp(b_end[d] - b[d])).astype(BF16)) for d in dirs]
        for d in dirs:
            st_scr[d] = st[d] * jnp.exp(b_end[d]) + upd[d]
    _scan_schedule(n_chunks, step)

    y = yf[...] + yb[...]
    yn = y * lax.rsqrt(jnp.mean(y * y, axis=-1, keepdims=True) + EPS)
    y_ref[...] = (yn * ng_ref[...] * _silu(g_ref[...].astype(F32))).astype(y_ref.dtype)


def gla_scan(p, low, conv_w, conv_b, gate_w, gate_b, norm_g, n_batch, n_chunks):
    rows = n_chunks * LC
    kq, kv, kg = B_HEADS, (2 * B_HEADS * B_DK) // B_DV, (2 * B_HEADS * B_DK + B_HEADS * B_DV) // B_DV
    masks = jnp.asarray(_gla_level_masks(), F32)
    cmap = lambda b, h: (0, 0, 0)
    return pl.pallas_call(
        functools.partial(_gla_kernel, n_chunks=n_chunks),
        grid=(n_batch, B_HEADS),
        in_specs=[pl.BlockSpec((rows, B_DK), lambda b, h: (b, h)),
                  pl.BlockSpec((rows, B_DK), lambda b, h: (b, kq + h)),
                  pl.BlockSpec((rows, B_DV), lambda b, h: (b, kv + h)),
                  pl.BlockSpec((rows, B_DV), lambda b, h: (b, kg + h)),
                  pl.BlockSpec((rows, 128), lambda b, h: (b, 0)),
                  pl.BlockSpec((3, B_DK), lambda b, h: (0, h)),
                  pl.BlockSpec((3, B_DK), lambda b, h: (0, kq + h)),
                  pl.BlockSpec((3, B_DV), lambda b, h: (0, kv + h)),
                  pl.BlockSpec((1, B_DK), lambda b, h: (0, h)),
                  pl.BlockSpec((1, B_DK), lambda b, h: (0, kq + h)),
                  pl.BlockSpec((1, B_DV), lambda b, h: (0, kv + h)),
                  pl.BlockSpec((2, 128, B_DK), lambda b, h: (0, 0, h)),
                  pl.BlockSpec((2, B_DK), lambda b, h: (0, h)),
                  pl.BlockSpec((1, B_DV), lambda b, h: (0, h)),
                  pl.BlockSpec((2 * GLA_LEVELS, LC, LC), cmap)],
        out_specs=pl.BlockSpec((rows, B_DV), lambda b, h: (b, h)),
        out_shape=jax.ShapeDtypeStruct((n_batch * rows, B_HEADS * B_DV), BF16),
        scratch_shapes=[pltpu.VMEM((rows, B_DK), BF16), pltpu.VMEM((rows, B_DK), BF16),
                        pltpu.VMEM((rows, B_DV), BF16),
                        pltpu.VMEM((rows, B_DV), F32), pltpu.VMEM((rows, B_DV), F32),
                        pltpu.VMEM((2, B_DV, B_DK), F32)],
        compiler_params=_cparams("parallel", "parallel"),
        name="gla_scan",
    )(p, p, p, p, low, conv_w, conv_w, conv_w, conv_b, conv_b, conv_b, gate_w, gate_b, norm_g, masks)


FB = 256


def _dft_mats(n_len):
    n = 2 * n_len
    nfb = n_len // FB
    f = np.arange(n_len)[:, None]
    t = np.arange(n_len)[None, :]
    ang = 2.0 * np.pi * ((f * t) % n) / n
    alt = (-1.0) ** np.arange(n_len)
    cf, sf = np.cos(ang), -np.sin(ang)
    sf[0, :] = alt
    fwd = np.stack([cf.reshape(nfb, FB, n_len), sf.reshape(nfb, FB, n_len)], axis=1).reshape(n, n_len)
    ci, si = (2.0 / n) * np.cos(ang.T), -(2.0 / n) * np.sin(ang.T)
    ci[:, 0] = 1.0 / n
    si[:, 0] = alt / n
    inv = np.stack([ci.reshape(n_len, nfb, FB), si.reshape(n_len, nfb, FB)], axis=2).reshape(n_len, n)
    return fwd.astype(np.float32), inv.astype(np.float32)


def _hy_pre_kernel(x0_ref, x1_ref, v_ref, cw0, cw1, cw2, cb0, cb1, cb2, zc_ref, zl_ref, x0c_ref, x0l_ref,
                   *, n_rows):
    x0 = _conv3(x0_ref, cw0, cb0, n_rows)
    z = _conv3(x1_ref, cw1, cb1, n_rows) * _conv3(v_ref, cw2, cb2, n_rows)
    zc_ref[...] = z[:LC].astype(BF16)
    zl_ref[...] = z[LC:].astype(BF16)
    x0c_ref[...] = x0[:LC].astype(BF16)
    x0l_ref[...] = x0[LC:].astype(BF16)


def hyena_pre(p, conv_w, conv_b, n_batch, n_chunks):
    rows = n_chunks * LC
    d = p.shape[1] // 3
    w = 256
    nj = d // w
    s_len = rows - LC
    spec = lambda k: pl.BlockSpec((rows, w), lambda b, j: (b, k * nj + j))
    cws = lambda k: pl.BlockSpec((3, w), lambda b, j: (0, k * nj + j))
    cbs = lambda k: pl.BlockSpec((1, w), lambda b, j: (0, k * nj + j))
    oc = pl.BlockSpec((LC, w), lambda b, j: (b, j))
    ol = pl.BlockSpec((s_len, w), lambda b, j: (b, j))
    return pl.pallas_call(
        functools.partial(_hy_pre_kernel, n_rows=rows),
        grid=(n_batch, nj),
        in_specs=[spec(0), spec(1), spec(2), cws(0), cws(1), cws(2), cbs(0), cbs(1), cbs(2)],
        out_specs=[oc, ol, oc, ol],
        out_shape=[jax.ShapeDtypeStruct((n_batch * LC, d), BF16), jax.ShapeDtypeStruct((n_batch * s_len, d), BF16),
                   jax.ShapeDtypeStruct((n_batch * LC, d), BF16), jax.ShapeDtypeStruct((n_batch * s_len, d), BF16)],
        compiler_params=_cparams("parallel", "parallel"),
        name="hyena_pre",
    )(p, p, p, conv_w, conv_w, conv_w, conv_b, conv_b, conv_b)


def _filt_dft_kernel(f_ref, x_ref, o_ref):
    acc = _dot(f_ref[...], x_ref[0])
    half = acc.shape[1] // 2
    o_ref[0] = acc[:, :half] + acc[:, half:]


def filter_spectrum(fwd, hsum, hdiff):
    n_len, d = hsum.shape

    def hilo(a):
        hi = a.astype(BF16)
        return jnp.concatenate([hi, (a - hi.astype(F32)).astype(BF16)], axis=1)
    xs = jnp.stack([hilo(hsum), hilo(hdiff)])
    return pl.pallas_call(
        _filt_dft_kernel,
        grid=(2, n_len // FB),
        in_specs=[pl.BlockSpec((FB, n_len), lambda w, m: (2 * m + w, 0)),
                  pl.BlockSpec((1, n_len, 2 * d), lambda w, m: (w, 0, 0))],
        out_specs=pl.BlockSpec((1, FB, d), lambda w, m: (w, m, 0)),
        out_shape=jax.ShapeDtypeStruct((2, n_len, d), F32),
        compiler_params=_cparams("parallel", "parallel"),
        name="filter_spectrum",
    )(fwd, xs)


def _dft_fwd_kernel(f_ref, z_ref, kr_ref, ki_ref, y_ref):
    acc = _dot(f_ref[...], z_ref[...])
    zr, zi = acc[:FB], acc[FB:]
    kr, ki = kr_ref[...], ki_ref[...]
    packed = jnp.logical_and(pl.program_id(1) == 0, lax.broadcasted_iota(I32, (FB, 1), 0) == 0)
    y_ref[:FB] = (zr * kr - jnp.where(packed, 0.0, zi * ki)).astype(BF16)
    y_ref[FB:] = jnp.where(packed, zi * ki, zr * ki + zi * kr).astype(BF16)


def dft_forward(fwd, z, kr, ki, n_batch):
    n_len, d = kr.shape
    nfb = n_len // FB
    return pl.pallas_call(
        _dft_fwd_kernel,
        grid=(n_batch, nfb),
        in_specs=[pl.BlockSpec((2 * FB, n_len), lambda b, m: (m, 0)),
                  pl.BlockSpec((n_len, d), lambda b, m: (b, 0)),
                  pl.BlockSpec((FB, d), lambda b, m: (m, 0)),
                  pl.BlockSpec((FB, d), lambda b, m: (m, 0))],
        out_specs=pl.BlockSpec((2 * FB, d), lambda b, m: (b * nfb + m, 0)),
        out_shape=jax.ShapeDtypeStruct((n_batch * 2 * n_len, d), BF16),
        compiler_params=_cparams("parallel", "arbitrary"),
        name="dft_forward",
    )(fwd, z, kr, ki)


def _dft_inv_kernel(ic_ref, yc_ref, zc_ref, x0c_ref, il_ref, yl_ref, zl_ref, x0l_ref, skip_ref, o_ref):
    def emit(i_ref, y_ref, z_ref, x0_ref):
        conv = _dot(i_ref[...], y_ref[...])
        o_ref[...] = (x0_ref[...].astype(F32) * (conv + z_ref[...].astype(F32) * skip_ref[...])).astype(o_ref.dtype)

    @pl.when(pl.program_id(1) == 0)
    def _():
        emit(ic_ref, yc_ref, zc_ref, x0c_ref)

    @pl.when(pl.program_id(1) > 0)
    def _():
        emit(il_ref, yl_ref, zl_ref, x0l_ref)


def dft_inverse(ctx_parts, lat_parts, skip, n_batch, tiles_per_batch):
    inv_c, yf_c, z_c, x0_c = ctx_parts
    inv_l, yf_l, z_l, x0_l = lat_parts
    d = z_l.shape[1]
    n_l = z_l.shape[0] // n_batch
    nt = n_l // TM
    lt = lambda m: jnp.maximum(m - 1, 0)
    return pl.pallas_call(
        _dft_inv_kernel,
        grid=(n_batch, tiles_per_batch),
        in_specs=[pl.BlockSpec((TM, 2 * LC), lambda b, m: (0, 0)),
                  pl.BlockSpec((2 * LC, d), lambda b, m: (b, 0)),
                  pl.BlockSpec((TM, d), lambda b, m: (b, 0)),
                  pl.BlockSpec((TM, d), lambda b, m: (b, 0)),
                  pl.BlockSpec((TM, 2 * n_l), lambda b, m: (lt(m), 0)),
                  pl.BlockSpec((2 * n_l, d), lambda b, m: (b, 0)),
                  pl.BlockSpec((TM, d), lambda b, m: (b * nt + lt(m), 0)),
                  pl.BlockSpec((TM, d), lambda b, m: (b * nt + lt(m), 0)),
                  pl.BlockSpec((1, d), lambda b, m: (0, 0))],
        out_specs=pl.BlockSpec((TM, d), lambda b, m: (b * tiles_per_batch + m, 0)),
        out_shape=jax.ShapeDtypeStruct((n_batch * tiles_per_batch * TM, d), BF16),
        compiler_params=_cparams("parallel", "arbitrary"),
        name="dft_inverse",
    )(inv_c, yf_c, z_c, x0_c, inv_l, yf_l, z_l, x0_l, skip.reshape(1, d))


def _hyena_filters(n_len, w1, b1, w_mid, b_mid, w_out, freq):
    hp = lax.Precision.HIGHEST
    d = w_out.shape[1] // 2
    n_bands = (w1.shape[0] - 1) // 2
    t = jnp.linspace(0.0, 1.0, n_len, dtype=F32)[:, None]
    pos = jnp.arange(n_len, dtype=F32)[:, None]
    bands = jnp.linspace(1e-4, n_bands - 1, n_bands, dtype=F32)[None, :]
    ang = (2.0 * math.pi / n_len) * pos * bands
    feats = jnp.concatenate([t, jnp.cos(ang), -jnp.sin(ang)], -1)
    h = jnp.sin(freq * (jnp.dot(feats, w1, precision=hp) + b1))
    for m in range(w_mid.shape[0]):
        h = jnp.sin(freq * (jnp.dot(h, w_mid[m], precision=hp) + b_mid[m]))
    h = jnp.dot(h, w_out, precision=hp).reshape(n_len, 2, d)
    deltas = jnp.abs(jnp.linspace(math.log(1e-2) / 1.5, math.log(1e-2) / 0.3, d, dtype=F32))
    h = h * jnp.exp(-t * deltas)[:, None, :]
    hf, hb = h[:, 0], h[:, 1]
    l1 = jnp.sum(jnp.abs(hf), 0) + jnp.sum(jnp.abs(hb[1:]), 0)
    return hf / l1, hb / l1


def hyena_mix(p, conv_w, conv_b, filt, skip, n_batch, n_chunks):
    rows = n_chunks * LC
    zc, zl, x0c, x0l = hyena_pre(p, conv_w, conv_b, n_batch, n_chunks)
    parts = []
    for z, x0 in ((zc, x0c), (zl, x0l)):
        n_len = z.shape[0] // n_batch
        fwd_np, inv_np = _dft_mats(n_len)
        fwd, inv = jnp.asarray(fwd_np, F32).astype(BF16), jnp.asarray(inv_np, F32).astype(BF16)
        hf, hb = _hyena_filters(n_len, *filt)
        hb0 = hb.at[0].set(0.0)
        hsum, hdiff = hf + hb0, hf - hb0
        spec = filter_spectrum(fwd, hsum, hdiff)
        alt = jnp.asarray((-1.0) ** np.arange(n_len), F32)[:, None]
        kr, ki = spec[0], spec[1].at[0].set(jnp.sum(alt * hsum, axis=0))
        parts.append((inv, dft_forward(fwd, z, kr, ki, n_batch), z, x0))
    return dft_inverse(parts[0], parts[1], skip, n_batch, n_chunks)


P_DTYPE = BF16
N_MAIN = 3072
N_EXTRA = 128


def _pad_cols(w, n):
    return jnp.pad(w, ((0, 0), (0, n - w.shape[1])))


def _to_scan_order(a, n_batch, ctx_len):
    w = a.shape[1]
    a3 = a.reshape(n_batch, -1, w)
    lat = a3[:, ctx_len:]
    s_len = lat.shape[1]
    lat = lat.reshape(n_batch, s_len // GRID_W, GRID_W, w).transpose(0, 2, 1, 3).reshape(n_batch, s_len, w)
    return jnp.concatenate([a3[:, :ctx_len], lat], axis=1).reshape(-1, w)


def _from_scan_order(a, n_batch, ctx_len):
    w = a.shape[1]
    a3 = a.reshape(n_batch, -1, w)
    lat = a3[:, ctx_len:]
    s_len = lat.shape[1]
    lat = lat.reshape(n_batch, GRID_W, s_len // GRID_W, w).transpose(0, 2, 1, 3).reshape(n_batch, s_len, w)
    return jnp.concatenate([a3[:, :ctx_len], lat], axis=1).reshape(-1, w)


def kernel(x, c, ctx, c_ctx, ada_w, ada_b, norm1_g, norm2_g, final_g, ml_w_in, ml_b_gate, ml_conv_w, ml_conv_b, ml_norm_g, ml_w_out, gla_w_in, gla_conv_w, gla_conv_b, gla_gate_w2, gla_gate_b, gla_norm_g, gla_w_out, hy_w_in, hy_b_in, hy_conv_w, hy_conv_b, hy_filt_w1, hy_filt_b1, hy_filt_w_mid, hy_filt_b_mid, hy_filt_w_out, hy_filt_freq, hy_skip, hy_w_out, hy_b_out, router_w, router_b, exp_w1, exp_w3, exp_w2):
    n_batch, s_len, d = x.shape
    ctx_len = ctx.shape[1]
    depth = ada_w.shape[0]
    assert ctx_len == LC == TM and s_len % LC == 0 and s_len % GRID_W == 0
    n_chunks = (ctx_len + s_len) // LC
    t_all = n_batch * n_chunks * LC
    lat_tiles = s_len // TM

    mod_rows = jnp.zeros((16, d), F32).at[0].set(c_ctx).at[1:1 + n_batch].set(c)
    mods = ada_mods(mod_rows, ada_w, ada_b).reshape(depth, 16, 6, d)
    xs = jnp.concatenate([ctx, x], axis=1).reshape(t_all, d)
    router_wt = router_w.T
    zeros_d = jnp.zeros((d,), F32)
    out = None

    for i in range(depth):
        last = i == depth - 1
        kind, j = i % 3, i // 3
        if kind == 0:
            w_in = _pad_cols(ml_w_in[j], N_MAIN + N_EXTRA).astype(BF16)
            bias = jnp.zeros((N_MAIN + N_EXTRA,), F32).at[N_MAIN:N_MAIN + 4 * A_HEADS].set(ml_b_gate[j])
            p, pg = in_proj(xs, mods[i], norm1_g[i], w_in, bias, N_MAIN, n_chunks, P_DTYPE)
            g4 = pg[:, :4 * A_HEADS].reshape(t_all, 4, A_HEADS // 2, 2)
            g_cols = g4.transpose(2, 0, 1, 3).reshape(A_HEADS // 2, t_all, 8)
            g_rows = g_cols.reshape(A_HEADS // 2, n_batch, n_chunks, LC, 8).transpose(0, 1, 2, 4, 3)
            y = mlstm_scan(p, g_rows, ml_conv_w[j], ml_conv_b[j].reshape(1, -1),
                           ml_norm_g[j].reshape(1, -1), n_batch, n_chunks)
            w_out, b_out = ml_w_out[j], zeros_d
        elif kind == 1:
            w_in = _pad_cols(gla_w_in[j], N_MAIN + N_EXTRA).astype(BF16)
            bias = jnp.zeros((N_MAIN + N_EXTRA,), F32)
            p, pg = in_proj(xs, mods[i], norm1_g[i], w_in, bias, N_MAIN, n_chunks, P_DTYPE)
            p = _to_scan_order(p, n_batch, ctx_len)
            pg = _to_scan_order(pg, n_batch, ctx_len)
            gate_w = jnp.zeros((2, 128, B_HEADS * B_DK), F32)
            gate_w = gate_w.at[0, :B_RANK].set(gla_gate_w2[j, 0]).at[1, B_RANK:2 * B_RANK].set(gla_gate_w2[j, 1])
            y = gla_scan(p, pg, gla_conv_w[j], gla_conv_b[j].reshape(1, -1), gate_w, gla_gate_b[j],
                         gla_norm_g[j].reshape(1, -1), n_batch, n_chunks)
            y = _from_scan_order(y, n_batch, ctx_len)
            w_out, b_out = gla_w_out[j], zeros_d
        else:
            (p,) = in_proj(xs, mods[i], norm1_g[i], hy_w_in[j].astype(BF16), hy_b_in[j], N_MAIN, n_chunks, P_DTYPE)
            filt = (hy_filt_w1[j], hy_filt_b1[j], hy_filt_w_mid[j], hy_filt_b_mid[j], hy_filt_w_out[j],
                    hy_filt_freq[j])
            y = hyena_mix(p, hy_conv_w[j], hy_conv_b[j].reshape(1, -1), filt, hy_skip[j], n_batch, n_chunks)
            w_out, b_out = hy_w_out[j], hy_b_out[j]

        x2, h2, ids, gates, ranks, counts = out_route(
            xs, y, mods[i], w_out.astype(BF16), b_out, norm2_g[i], router_wt, router_b, n_chunks, 1, last)
        n_tok = x2.shape[0]
        dest_tiles, n_slots, block_exp, n_used = moe_plan(ids, ranks, counts, n_tok)
        xb = moe_dispatch(h2, dest_tiles, n_slots)
        yb = moe_experts(xb, block_exp, n_used, exp_w1, exp_w3, exp_w2, i)
        if last:
            out = moe_combine(x2, yb, dest_tiles, gates.T, mods[i], lambda t: 1 + t // lat_tiles, final_g)
        else:
            xs = moe_combine(x2, yb, dest_tiles, gates.T, mods[i],
                             lambda t: jnp.where(t % n_chunks == 0, 0, 1 + t // n_chunks))
    return out.reshape(n_batch, s_len, d)
```

```python
import functools
import math

import numpy as np
import jax
import jax.numpy as jnp
from jax import lax
from jax.experimental import pallas as pl
from jax.experimental.pallas import tpu as pltpu

F32 = jnp.float32
BF16 = jnp.bfloat16
I32 = jnp.int32

EPS = 1e-6
TM = 256
LC = 256
MOE_BLOCK = 512
N_EXPERTS = 16
N_GROUPS = 4
EPG = N_EXPERTS // N_GROUPS
A_HEADS, A_DK, A_DV = 8, 64, 128
B_HEADS, B_DK, B_DV = 4, 128, 256
B_RANK = 16
B_TAU = 16.0
GRID_W = 64
VMEM_LIMIT_V7X = 56 * 1024 * 1024


def _cparams(*sem):
    return pltpu.CompilerParams(dimension_semantics=sem, vmem_limit_bytes=VMEM_LIMIT_V7X)


def _silu(x):
    return x * (1.0 / (1.0 + jnp.exp(-x)))


def _sigmoid(x):
    return 1.0 / (1.0 + jnp.exp(-x))


def _log_sigmoid(x):
    return jnp.minimum(x, 0.0) - jnp.log(1.0 + jnp.exp(-jnp.abs(x)))


def _split3(x):
    hi = x.astype(BF16)
    r1 = x - hi.astype(F32)
    mid = r1.astype(BF16)
    lo = (r1 - mid.astype(F32)).astype(BF16)
    return hi, mid, lo


def _dot(a, b):
    return jnp.dot(a, b, preferred_element_type=F32)


def _dot_nt(a, b):
    return lax.dot_general(a, b, (((1,), (1,)), ((), ())), preferred_element_type=F32)


def _dot_tn(a, b):
    return lax.dot_general(a, b, (((0,), (0,)), ((), ())), preferred_element_type=F32)


LANES = 128
SUBLANES = 8


def _tok_tiles_store(ref, base, x):
    n, d = x.shape
    per = d // LANES
    for c in range(per):
        ref[pl.ds(base + c, n, stride=per), :] = x[:, c * LANES:(c + 1) * LANES]


def _tok_tiles_load(ref, base, n, d):
    per = d // LANES
    return jnp.concatenate([ref[pl.ds(base + c, n, stride=per), :] for c in range(per)], axis=1)


def _dot_exact_lhs01(m01, x):
    hi, mid, lo = _split3(x)
    return (_dot(m01, hi) + _dot(m01, mid)) + _dot(m01, lo)


def _dot_exact_rhs01(x, m01):
    hi, mid, lo = _split3(x)
    return (_dot(hi, m01) + _dot(mid, m01)) + _dot(lo, m01)


def _ada_kernel(s_ref, w_ref, b_ref, o_ref):
    s = _silu(s_ref[...]).astype(BF16)
    o_ref[0] = _dot(s, w_ref[0].astype(BF16)) + b_ref[0]


def ada_mods(rows, ada_w, ada_b):
    depth, d, n6 = ada_w.shape
    nr = rows.shape[0]
    tn = 1536
    return pl.pallas_call(
        _ada_kernel,
        grid=(depth, n6 // tn),
        in_specs=[pl.BlockSpec((nr, d), lambda l, j: (0, 0)),
                  pl.BlockSpec((1, d, tn), lambda l, j: (l, 0, j)),
                  pl.BlockSpec((1, 1, tn), lambda l, j: (l, 0, j))],
        out_specs=pl.BlockSpec((1, nr, tn), lambda l, j: (l, 0, j)),
        out_shape=jax.ShapeDtypeStruct((depth, nr, n6), F32),
        compiler_params=_cparams("parallel", "parallel"),
        name="ada_mods",
    )(rows, ada_w, ada_b.reshape(depth, 1, n6))


def _norm_mod(x, g, shift, scale):
    y = x * lax.rsqrt(jnp.mean(x * x, axis=-1, keepdims=True) + EPS) * g
    return y * (1.0 + scale) + shift


def _in_proj_kernel(x_ref, mod_ref, g_ref, w_ref, b_ref, p_ref, *rest, n_main):
    h = _norm_mod(x_ref[...], g_ref[...], mod_ref[0, 0:1, :], mod_ref[0, 1:2, :]).astype(BF16)
    p_ref[...] = (_dot(h, w_ref[:, :n_main]) + b_ref[:, :n_main]).astype(p_ref.dtype)
    if rest:
        rest[0][...] = _dot(h, w_ref[:, n_main:]) + b_ref[:, n_main:]


def in_proj(x, mods_l, g, w, b, n_main, tiles_per_batch, p_dtype):
    t, d = x.shape
    n = w.shape[1]
    n_extra = n - n_main
    mod_row = lambda i: jnp.where(i % tiles_per_batch == 0, 0, 1 + i // tiles_per_batch)
    out_shape = [jax.ShapeDtypeStruct((t, n_main), p_dtype)]
    out_specs = [pl.BlockSpec((TM, n_main), lambda i: (i, 0))]
    if n_extra:
        out_shape.append(jax.ShapeDtypeStruct((t, n_extra), F32))
        out_specs.append(pl.BlockSpec((TM, n_extra), lambda i: (i, 0)))
    return pl.pallas_call(
        functools.partial(_in_proj_kernel, n_main=n_main),
        grid=(t // TM,),
        in_specs=[pl.BlockSpec((TM, d), lambda i: (i, 0)),
                  pl.BlockSpec((1, 6, d), lambda i: (mod_row(i), 0, 0)),
                  pl.BlockSpec((1, d), lambda i: (0, 0)),
                  pl.BlockSpec((d, n), lambda i: (0, 0)),
                  pl.BlockSpec((1, n), lambda i: (0, 0))],
        out_specs=out_specs,
        out_shape=out_shape,
        compiler_params=_cparams("parallel"),
        name="in_proj",
    )(x, mods_l, g.reshape(1, d), w, b.reshape(1, n))


def _first_argmax(vals):
    best, idx = vals[0], jnp.zeros(vals[0].shape, I32)
    for i in range(1, len(vals)):
        take = vals[i] > best
        best = jnp.where(take, vals[i], best)
        idx = jnp.where(take, i, idx)
    return best, idx


def _select_row(rows, idx):
    out = rows[0]
    for i in range(1, len(rows)):
        out = jnp.where(idx == i, rows[i], out)
    return out


def _route(scores, sel):
    neg = jnp.float32(-jnp.inf)
    srow = [sel[e:e + 1, :] for e in range(N_EXPERTS)]
    crow = [scores[e:e + 1, :] for e in range(N_EXPERTS)]
    gscore = []
    for g in range(N_GROUPS):
        a = srow[g * EPG:(g + 1) * EPG]
        m1, i1 = _first_argmax(a)
        m2, _ = _first_argmax([jnp.where(i1 == i, neg, a[i]) for i in range(EPG)])
        gscore.append(m1 + m2)
    _, grp = _first_argmax(gscore)
    in_sel = [_select_row([srow[g * EPG + i] for g in range(N_GROUPS)], grp) for i in range(EPG)]
    in_sc = [_select_row([crow[g * EPG + i] for g in range(N_GROUPS)], grp) for i in range(EPG)]
    _, l1 = _first_argmax(in_sel)
    _, l2 = _first_argmax([jnp.where(l1 == i, neg, in_sel[i]) for i in range(EPG)])
    g1 = _select_row(in_sc, l1)
    g2 = _select_row(in_sc, l2)
    tot = g1 + g2
    ids = jnp.concatenate([grp * EPG + l1, grp * EPG + l2], axis=0)
    gates = jnp.concatenate([g1 / tot, g2 / tot], axis=0)
    return ids, gates


def _out_route_kernel(xa_ref, xb_ref, ya_ref, yb_ref, mod_ref, w_ref, b_ref, g_ref, rw_ref, rb_ref,
                      xo_ref, h_ref, ids_ref, gates_ref, rank_ref, cnt_ref, carry_ref, *, mod_row):
    i = pl.program_id(0)

    @pl.when(i == 0)
    def _():
        carry_ref[...] = jnp.zeros_like(carry_ref)

    halves = (0, 1)
    rows = [slice(k * TM, (k + 1) * TM) for k in halves]
    mod = [mod_ref[mod_row(2 * i + k)] for k in halves]
    proj = [_dot(y_ref[...], w_ref[...]) for y_ref in (ya_ref, yb_ref)]
    xn = [x_ref[...] + mod[k][2:3, :] * (proj[k] + b_ref[...]) for k, x_ref in enumerate((xa_ref, xb_ref))]
    h = [_norm_mod(xn[k], g_ref[...], mod[k][3:4, :], mod[k][4:5, :]) for k in halves]
    logits = [lax.dot_general(rw_ref[...], h[k], (((1,), (1,)), ((), ())),
                              precision=lax.Precision.HIGHEST, preferred_element_type=F32) for k in halves]
    for k in halves:
        xo_ref[rows[k], :] = xn[k]
        _tok_tiles_store(h_ref, k * TM * SUBLANES, h[k])
    scores = [_sigmoid(logits[k]) for k in halves]
    routes = [_route(scores[k], scores[k] + rb_ref[:, 0:1]) for k in halves]
    erow = lax.broadcasted_iota(I32, (N_EXPERTS, TM), 0)
    before = (lax.broadcasted_iota(I32, (TM, TM), 0) < lax.broadcasted_iota(I32, (TM, TM), 1)).astype(BF16)
    oh = [[(erow == routes[k][0][j:j + 1, :]).astype(F32) for j in range(2)] for k in halves]
    pre = [[_dot(oh[k][j].astype(BF16), before) for j in range(2)] for k in halves]
    tot = [[jnp.sum(oh[k][j], axis=1, keepdims=True) for j in range(2)] for k in halves]

    carry = carry_ref[:, 0:1]
    for k in halves:
        ids_ref[:, rows[k]] = routes[k][0]
        gates_ref[:, rows[k]] = routes[k][1]
        r0 = jnp.sum(oh[k][0] * (carry + pre[k][0]), axis=0, keepdims=True)
        r1 = jnp.sum(oh[k][1] * (carry + tot[k][0] + pre[k][1]), axis=0, keepdims=True)
        rank_ref[:, rows[k]] = jnp.concatenate([r0, r1], axis=0).astype(I32)
        carry = carry + tot[k][0] + tot[k][1]
    carry_ref[...] = jnp.broadcast_to(carry, carry_ref.shape)
    cnt_ref[...] = jnp.broadcast_to(carry, cnt_ref.shape).astype(I32)


def out_route(x, y, mods_l, w_out, b_out, g2n, router_wt, router_b, tiles_per_batch, ctx_tiles, skip_ctx):
    t, d = x.shape
    nb = t // (tiles_per_batch * TM)
    if skip_ctx:
        per = tiles_per_batch - ctx_tiles
        src = lambda i: (i // per) * tiles_per_batch + ctx_tiles + i % per
        mod_row = lambda i: 1 + i // per
        n_tiles = nb * per
    else:
        src = lambda i: i
        mod_row = lambda i: jnp.where(i % tiles_per_batch < ctx_tiles, 0, 1 + i // tiles_per_batch)
        n_tiles = nb * tiles_per_batch
    assert n_tiles % 2 == 0
    tr = n_tiles * TM
    tok = lambda i: (i, 0)
    row2 = lambda i: (0, i)
    outs = pl.pallas_call(
        functools.partial(_out_route_kernel, mod_row=mod_row),
        grid=(n_tiles // 2,),
        in_specs=[pl.BlockSpec((TM, d), lambda i: (src(2 * i), 0)),
                  pl.BlockSpec((TM, d), lambda i: (src(2 * i + 1), 0)),
                  pl.BlockSpec((TM, d), lambda i: (src(2 * i), 0)),
                  pl.BlockSpec((TM, d), lambda i: (src(2 * i + 1), 0)),
                  pl.BlockSpec(mods_l.shape, lambda i: (0, 0, 0)),
                  pl.BlockSpec((d, d), lambda i: (0, 0)),
                  pl.BlockSpec((1, d), lambda i: (0, 0)),
                  pl.BlockSpec((1, d), lambda i: (0, 0)),
                  pl.BlockSpec((N_EXPERTS, d), lambda i: (0, 0)),
                  pl.BlockSpec((N_EXPERTS, 128), lambda i: (0, 0))],
        out_specs=[pl.BlockSpec((2 * TM, d), tok),
                   pl.BlockSpec((2 * TM * SUBLANES, LANES), tok),
                   pl.BlockSpec((2, 2 * TM), row2),
                   pl.BlockSpec((2, 2 * TM), row2),
                   pl.BlockSpec((2, 2 * TM), row2),
                   pl.BlockSpec((N_EXPERTS, 128), lambda i: (0, 0))],
        out_shape=[jax.ShapeDtypeStruct((tr, d), F32),
                   jax.ShapeDtypeStruct((tr * SUBLANES, LANES), F32),
                   jax.ShapeDtypeStruct((2, tr), I32),
                   jax.ShapeDtypeStruct((2, tr), F32),
                   jax.ShapeDtypeStruct((2, tr), I32),
                   jax.ShapeDtypeStruct((N_EXPERTS, 128), I32)],
        scratch_shapes=[pltpu.VMEM((N_EXPERTS, 128), F32)],
        compiler_params=_cparams("arbitrary"),
        name="out_route",
    )(x, x, y, y, mods_l, w_out, b_out.reshape(1, d), g2n.reshape(1, d), router_wt,
      jnp.broadcast_to(router_b.reshape(N_EXPERTS, 1), (N_EXPERTS, 128)))
    return outs


def _moe_kernel(bexp_ref, nused_ref, x_ref, w1_ref, w3_ref, w2_ref, o_ref, w1b, w3b, w2b):
    i = pl.program_id(0)
    prev = bexp_ref[jnp.maximum(i - 1, 0)]
    fresh = jnp.logical_or(i == 0, bexp_ref[i] != prev)
    used = i < nused_ref[0]

    @pl.when(jnp.logical_and(fresh, used))
    def _():
        w1b[...] = w1_ref[0].astype(BF16)
        w3b[...] = w3_ref[0].astype(BF16)
        w2b[...] = w2_ref[0].astype(BF16)

    @pl.when(used)
    def _():
        x = _tok_tiles_load(x_ref, 0, MOE_BLOCK, w1b.shape[0]).astype(BF16)
        a = _dot(x, w1b[...])
        b = _dot(x, w3b[...])
        hmid = (_silu(a) * b).astype(BF16)
        _tok_tiles_store(o_ref, 0, _dot(hmid, w2b[...]))

    @pl.when(jnp.logical_not(used))
    def _():
        o_ref[...] = jnp.zeros_like(o_ref)


def moe_experts(xb, block_exp, n_used, w1, w3, w2, layer):
    d, ff = w1.shape[2], w1.shape[3]
    per = d // LANES
    n_slots, dh = xb.shape[0] // per, LANES
    n_blocks = n_slots // MOE_BLOCK
    wmap = lambda i, be, nu: (layer, be[i], 0, 0)
    return pl.pallas_call(
        _moe_kernel,
        grid_spec=pltpu.PrefetchScalarGridSpec(
            num_scalar_prefetch=2,
            grid=(n_blocks,),
            in_specs=[pl.BlockSpec((MOE_BLOCK * per, dh), lambda i, be, nu: (i, 0)),
                      pl.BlockSpec((None, 1, d, ff), wmap),
                      pl.BlockSpec((None, 1, d, ff), wmap),
                      pl.BlockSpec((None, 1, ff, d), wmap)],
            out_specs=pl.BlockSpec((MOE_BLOCK * per, dh), lambda i, be, nu: (i, 0)),
            scratch_shapes=[pltpu.VMEM((d, ff), BF16), pltpu.VMEM((d, ff), BF16),
                            pltpu.VMEM((ff, d), BF16)]),
        out_shape=jax.ShapeDtypeStruct((n_slots * per, dh), F32),
        compiler_params=_cparams("arbitrary"),
        name="moe_experts",
    )(block_exp, n_used, xb, w1, w3, w2)


def moe_plan(ids, ranks, counts, n_tokens):
    n_assign = 2 * n_tokens
    n_blocks = n_assign // MOE_BLOCK + N_EXPERTS
    cnt = counts[:, 0]
    padded = (cnt + MOE_BLOCK - 1) // MOE_BLOCK * MOE_BLOCK
    pad_end = jnp.cumsum(padded)
    pad_start = pad_end - padded
    start_of = sum(jnp.where(ids == e, pad_start[e], 0) for e in range(N_EXPERTS))
    dest = start_of + ranks
    blk_start = jnp.arange(n_blocks, dtype=I32) * MOE_BLOCK
    block_exp = jnp.minimum(jnp.sum(blk_start[:, None] >= pad_end[None, :], axis=1), N_EXPERTS - 1).astype(I32)
    n_used = (pad_end[-1] // MOE_BLOCK).astype(I32).reshape(1)
    dest_tiles = dest.reshape(2, n_tokens // TM, TM).transpose(1, 0, 2)
    return dest_tiles, n_blocks * MOE_BLOCK, block_exp, n_used, pad_end.astype(I32)


def _tile_of(idx):
    return pl.ds(pl.multiple_of(idx * SUBLANES, SUBLANES), SUBLANES)


def _token_copies(src_of, dst_of, dest_ref, sem):
    def body(r, carry):
        for k in range(2):
            pltpu.make_async_copy(src_of(k, r, dest_ref[0, k, r]), dst_of(k, r, dest_ref[0, k, r]), sem).start()
        return carry
    lax.fori_loop(0, TM, body, 0, unroll=8)


def _dispatch_kernel(pad_end_ref, nused_ref, dest_ref, h_ref, xb_ref, zeros, ring, sem, zsem, *, n_blocks):
    i = pl.program_id(0)
    n = pl.num_programs(0)
    blk_rows = MOE_BLOCK * SUBLANES

    def zero_block(blk):
        return pltpu.make_async_copy(zeros, xb_ref.at[pl.ds(pl.multiple_of(blk * blk_rows, blk_rows), blk_rows)],
                                     zsem)

    @pl.when(i == 0)
    def _():
        zeros[...] = jnp.zeros_like(zeros)
        fills = []
        for e in range(N_EXPERTS):
            end = pad_end_ref[e]
            start = pad_end_ref[e - 1] if e else 0
            fills.append((end > start, end // MOE_BLOCK - 1))
        for j in range(N_EXPERTS):
            fills.append((nused_ref[0] + j < n_blocks, nused_ref[0] + j))
        for cond, blk in fills:
            @pl.when(cond)
            def _():
                zero_block(blk).start()
        for cond, blk in fills:
            @pl.when(cond)
            def _():
                zero_block(blk).wait()

    def tile_wait(slot):
        for _ in range(2):
            pltpu.make_async_copy(h_ref, xb_ref.at[pl.ds(0, TM * SUBLANES)], sem.at[slot]).wait()

    slot = i % 2

    @pl.when(i >= 2)
    def _():
        tile_wait(slot)
    ring[slot] = h_ref[...]
    _token_copies(lambda k, r, s: ring.at[slot, _tile_of(r)], lambda k, r, s: xb_ref.at[_tile_of(s)], dest_ref,
                  sem.at[slot])

    @pl.when(i == n - 1)
    def _():
        @pl.when(n >= 2)
        def _():
            tile_wait(1 - slot)
        tile_wait(slot)


def moe_dispatch(h, dest_tiles, pad_end, n_used, n_slots):
    rows, dh = h.shape
    return pl.pallas_call(
        functools.partial(_dispatch_kernel, n_blocks=n_slots // MOE_BLOCK),
        grid_spec=pltpu.PrefetchScalarGridSpec(
            num_scalar_prefetch=2,
            grid=(rows // (TM * SUBLANES),),
            in_specs=[pl.BlockSpec((1, 2, TM), lambda i, pe, nu: (i, 0, 0), memory_space=pltpu.SMEM),
                      pl.BlockSpec((TM * SUBLANES, dh), lambda i, pe, nu: (i, 0))],
            out_specs=pl.BlockSpec(memory_space=pl.ANY),
            scratch_shapes=[pltpu.VMEM((MOE_BLOCK * SUBLANES, dh), h.dtype),
                            pltpu.VMEM((2, TM * SUBLANES, dh), h.dtype),
                            pltpu.SemaphoreType.DMA((2,)), pltpu.SemaphoreType.DMA(())]),
        out_shape=jax.ShapeDtypeStruct((n_slots * SUBLANES, dh), h.dtype),
        compiler_params=_cparams("arbitrary"),
        name="moe_dispatch",
    )(pad_end, n_used, dest_tiles, h)


def _combine_kernel(dcur_ref, dnext_ref, x_ref, yb_ref, gt_ref, mod_ref, *rest, final):
    buf, sem = rest[-2:]
    i = pl.program_id(0)
    n = pl.num_programs(0)

    def fetch(dest_ref, slot):
        _token_copies(lambda k, r, s: yb_ref.at[_tile_of(s)], lambda k, r, s: buf.at[slot, k, _tile_of(r)],
                      dest_ref, sem.at[slot])

    @pl.when(i == 0)
    def _():
        fetch(dcur_ref, 0)

    @pl.when(i + 1 < n)
    def _():
        fetch(dnext_ref, (i + 1) % 2)

    slot = i % 2
    for k in range(2):
        pltpu.make_async_copy(yb_ref.at[pl.ds(0, TM * SUBLANES)], buf.at[slot, k], sem.at[slot]).wait()
    d = x_ref.shape[1]
    moe = (gt_ref[:, 0:1] * _tok_tiles_load(buf.at[slot, 0], 0, TM, d)
           + gt_ref[:, 1:2] * _tok_tiles_load(buf.at[slot, 1], 0, TM, d))
    xn = x_ref[...] + mod_ref[0, 5:6, :] * moe
    if final:
        g_ref, o_ref = rest[:2]
        o_ref[...] = xn * lax.rsqrt(jnp.mean(xn * xn, axis=-1, keepdims=True) + EPS) * g_ref[...]
    else:
        rest[0][...] = xn


def moe_combine(x, yb, dest_tiles, gates_t, mods_l, mod_row, final_g=None):
    t, d = x.shape
    n_tiles = t // TM
    tok = lambda i: (i, 0)
    in_specs = [pl.BlockSpec((1, 2, TM), lambda i: (i, 0, 0), memory_space=pltpu.SMEM),
                pl.BlockSpec((1, 2, TM), lambda i: (jnp.minimum(i + 1, n_tiles - 1), 0, 0), memory_space=pltpu.SMEM),
                pl.BlockSpec((TM, d), tok),
                pl.BlockSpec(memory_space=pl.ANY),
                pl.BlockSpec((TM, 2), tok),
                pl.BlockSpec((1, 6, d), lambda i: (mod_row(i), 0, 0))]
    args = [dest_tiles, dest_tiles, x, yb, gates_t, mods_l]
    if final_g is not None:
        in_specs.append(pl.BlockSpec((1, d), lambda i: (0, 0)))
        args.append(final_g.reshape(1, d))
    return pl.pallas_call(
        functools.partial(_combine_kernel, final=final_g is not None),
        grid=(n_tiles,),
        in_specs=in_specs,
        out_specs=pl.BlockSpec((TM, d), tok),
        out_shape=jax.ShapeDtypeStruct((t, d), F32),
        scratch_shapes=[pltpu.VMEM((2, 2, TM * SUBLANES, LANES), F32), pltpu.SemaphoreType.DMA((2,))],
        compiler_params=_cparams("arbitrary"),
        name="moe_combine",
    )(*args)


def _conv3(x_ref, cw_ref, cb_ref, n_rows):
    x = x_ref[...].astype(F32)
    row = lax.broadcasted_iota(I32, (n_rows, 1), 0)
    first = jnp.logical_or(row == 0, row == LC)
    last = jnp.logical_or(row == LC - 1, row == n_rows - 1)
    prev = jnp.where(first, 0.0, pltpu.roll(x, 1, 0))
    nxt = jnp.where(last, 0.0, pltpu.roll(x, n_rows - 1, 0))
    return prev * cw_ref[0:1, :] + x * cw_ref[1:2, :] + nxt * cw_ref[2:3, :] + cb_ref[...]


def _conv_silu(x_ref, cw_ref, cb_ref, n_rows):
    return _silu(_conv3(x_ref, cw_ref, cb_ref, n_rows))


def _tri_masks():
    ri = lax.broadcasted_iota(I32, (LC, LC), 0)
    ci = lax.broadcasted_iota(I32, (LC, LC), 1)
    return ri >= ci, ri <= ci


def _scan_schedule(n_chunks, step):
    step(0, 0)

    def body(t, carry):
        step(t, n_chunks - t)
        return carry
    lax.fori_loop(1, n_chunks, body, 0)


def _rows16(rows):
    n = rows[0].shape[1]
    ri = lax.broadcasted_iota(I32, (16, n), 0)
    out = jnp.zeros((16, n), F32)
    for i, r in enumerate(rows):
        out = jnp.where(ri == i, jnp.broadcast_to(r, (16, n)), out)
    return out.astype(BF16)


def _mlstm_kernel(q_ref, k_ref, v_ref, o_ref, gt_ref, cwq_ref, cwk_ref, cbq_ref, cbk_ref, ng_ref,
                  y_ref, qs, ks, kst, vt, yft, ybt, ct_scr, m_scr, *, n_chunks):
    n_rows = n_chunks * LC
    qs[...] = (_conv_silu(q_ref, cwq_ref, cbq_ref, n_rows) * (A_DK ** -0.5)).astype(BF16)
    kf = _conv_silu(k_ref, cwk_ref, cbk_ref, n_rows)
    ks[...] = kf.astype(BF16)
    for c in range(n_chunks):
        kst[c] = kf[c * LC:(c + 1) * LC, :].T.astype(BF16)
        vt[c] = v_ref[c * LC:(c + 1) * LC, :].astype(F32).T.astype(BF16)
    ct_scr[...] = jnp.zeros_like(ct_scr)
    m_scr[...] = jnp.zeros_like(m_scr)
    tril, triu = _tri_masks()
    tril_b, triu_b = tril.astype(BF16), triu.astype(BF16)
    ones_rows = (lax.broadcasted_iota(I32, (A_DV, LC), 0) == 0).astype(BF16)
    one = jnp.ones((1, LC), F32)
    neg = jnp.float32(-jnp.inf)

    f32 = lambda t: t.astype(F32)

    def step(cf, cb):
        chains = [(cf, True, 0), (cf, True, 1), (cb, False, 0), (cb, False, 1)]
        m_prev = [m_scr[i, 0:1, 0:1] for i in range(4)]
        ct = [ct_scr[i] for i in range(4)]
        gts = {True: gt_ref[0, 0, cf], False: gt_ref[0, 0, cb]}
        css = {True: _dot_exact_rhs01(_log_sigmoid(gts[True]), triu_b),
               False: _dot_exact_rhs01(_log_sigmoid(gts[False]), tril_b)}
        rows = [pl.ds(pl.multiple_of(c * LC, LC), LC) for c, _, _ in chains]
        qh = [qs[rows[i], hh * A_DK:(hh + 1) * A_DK] for i, (_, _, hh) in enumerate(chains)]
        kh = [ks[rows[i], hh * A_DK:(hh + 1) * A_DK] for i, (_, _, hh) in enumerate(chains)]
        qk_t = [_dot_nt(kh[i], qh[i]) for i in range(4)]
        fcum, g, log_d = [], [], []
        for c, fwd, hh in chains:
            d = 0 if fwd else 1
            fc_row = css[fwd][4 * d + 2 + hh:4 * d + 3 + hh, :]
            g_row = gts[fwd][4 * d + hh:4 * d + hh + 1, :] - fc_row
            f_hi, f_mid, f_lo = _split3(fc_row)
            g_hi, g_mid, g_lo = _split3(g_row)
            f_slab = _rows16([f32(f_hi), f32(f_mid), f32(f_lo), one, one, one])
            g_slab = _rows16([one, one, one, f32(g_hi), f32(g_mid), f32(g_lo)])
            log_d.append(jnp.where(triu if fwd else tril, _dot_tn(g_slab, f_slab), neg))
            fcum.append(fc_row)
            g.append(g_row)
        li = [fcum[i] + m_prev[i] for i in range(4)]
        m_row = [jnp.maximum(li[i], jnp.max(log_d[i], axis=0, keepdims=True)) for i in range(4)]
        sm_t = [(qk_t[i] * jnp.exp(log_d[i] - m_row[i])).astype(BF16) for i in range(4)]
        vext_t = [jnp.concatenate([vt[c, hh * A_DV:(hh + 1) * A_DV, :], ones_rows], axis=0)
                  for c, _, hh in chains]
        intra = [_dot(vext_t[i], sm_t[i]) for i in range(4)]
        inter = [_dot_nt(ct[i].astype(BF16), qh[i]) for i in range(4)]
        kw_t, m_new = [], []
        for i, (c, fwd, hh) in enumerate(chains):
            num_t = intra[i] + jnp.exp(li[i] - m_row[i]) * inter[i]
            den = num_t[A_DV:A_DV + 1, :]
            (yft if fwd else ybt)[c, hh * A_DV:(hh + 1) * A_DV, :] = num_t[:A_DV, :] * (
                1.0 / jnp.maximum(jnp.abs(den), jnp.exp(-m_row[i])))
            b_end = fcum[i][:, LC - 1:LC] if fwd else fcum[i][:, 0:1]
            m_new.append(jnp.maximum(b_end + m_prev[i], jnp.max(b_end + g[i], axis=1, keepdims=True)))
            kw_t.append((kst[c, hh * A_DK:(hh + 1) * A_DK, :].astype(F32)
                         * jnp.exp(b_end + g[i] - m_new[i])).astype(BF16))
            ct[i] = jnp.exp(b_end + m_prev[i] - m_new[i]) * ct[i]
        upd = [_dot_nt(vext_t[i], kw_t[i]) for i in range(4)]
        for i in range(4):
            ct_scr[i] = ct[i] + upd[i]
            m_scr[i] = jnp.broadcast_to(m_new[i], m_scr.shape[1:])
    _scan_schedule(n_chunks, step)

    for hh in range(2):
        sl = slice(hh * A_DV, (hh + 1) * A_DV)
        for c in range(n_chunks):
            y_t = yft[c, sl, :] + ybt[c, sl, :]
            yn = (y_t * lax.rsqrt(jnp.mean(y_t * y_t, axis=0, keepdims=True) + EPS)).T
            rs = slice(c * LC, (c + 1) * LC)
            y_ref[rs, sl] = (yn * ng_ref[:, sl] * _sigmoid(o_ref[rs, sl].astype(F32))).astype(y_ref.dtype)


def mlstm_scan(p, g_rows, conv_w, conv_b, norm_g, n_batch, n_chunks):
    rows = n_chunks * LC
    n_hp = A_HEADS // 2
    qk_w = 2 * A_DK
    v_w = 2 * A_DV
    kq, kv, ko = (A_HEADS * A_DK) // qk_w, (2 * A_HEADS * A_DK) // v_w, (2 * A_HEADS * A_DK + A_HEADS * A_DV) // v_w
    return pl.pallas_call(
        functools.partial(_mlstm_kernel, n_chunks=n_chunks),
        grid=(n_batch, n_hp),
        in_specs=[pl.BlockSpec((rows, qk_w), lambda b, h: (b, h)),
                  pl.BlockSpec((rows, qk_w), lambda b, h: (b, kq + h)),
                  pl.BlockSpec((rows, v_w), lambda b, h: (b, kv + h)),
                  pl.BlockSpec((rows, v_w), lambda b, h: (b, ko + h)),
                  pl.BlockSpec((1, 1, n_chunks, 8, LC), lambda b, h: (h, b, 0, 0, 0)),
                  pl.BlockSpec((3, qk_w), lambda b, h: (0, h)),
                  pl.BlockSpec((3, qk_w), lambda b, h: (0, kq + h)),
                  pl.BlockSpec((1, qk_w), lambda b, h: (0, h)),
                  pl.BlockSpec((1, qk_w), lambda b, h: (0, kq + h)),
                  pl.BlockSpec((1, v_w), lambda b, h: (0, h))],
        out_specs=pl.BlockSpec((rows, v_w), lambda b, h: (b, h)),
        out_shape=jax.ShapeDtypeStruct((n_batch * rows, A_HEADS * A_DV), BF16),
        scratch_shapes=[pltpu.VMEM((rows, qk_w), BF16), pltpu.VMEM((rows, qk_w), BF16),
                        pltpu.VMEM((n_chunks, qk_w, LC), BF16), pltpu.VMEM((n_chunks, v_w, LC), BF16),
                        pltpu.VMEM((n_chunks, v_w, LC), F32), pltpu.VMEM((n_chunks, v_w, LC), F32),
                        pltpu.VMEM((4, 2 * A_DV, A_DK), F32), pltpu.VMEM((4, 8, 128), F32)],
        compiler_params=_cparams("parallel", "parallel"),
        name="mlstm_scan",
    )(p, p, p, p, g_rows, conv_w, conv_w, conv_b, conv_b, norm_g)


GLA_LEVELS = int(math.log2(LC))


def _gla_level_masks():
    j = np.arange(LC)[:, None]
    s = np.arange(LC)[None, :]
    out = []
    for fwd in (True, False):
        for lev in range(GLA_LEVELS):
            m = 1 << lev
            same = (j // (2 * m)) == (s // (2 * m))
            hi_j, hi_s = (j % (2 * m)) >= m, (s % (2 * m)) >= m
            out.append(same & (hi_j & ~hi_s if fwd else ~hi_j & hi_s))
    return np.stack(out).astype(np.float32)


def _seg_ref(b, m, fwd):
    n, w = b.shape
    r = m - 1 if fwd else m
    if 2 * m >= 8:
        b3 = b.reshape(n // (2 * m), 2 * m, w)
        return jnp.broadcast_to(b3[:, r:r + 1, :], b3.shape).reshape(n, w)
    b3 = b.reshape(n // 8, 8, w)
    sub = lax.broadcasted_iota(I32, b3.shape, 1)
    out = None
    for blk in range(8 // (2 * m)):
        row = jnp.broadcast_to(b3[:, blk * 2 * m + r:blk * 2 * m + r + 1, :], b3.shape)
        out = row if out is None else jnp.where(sub >= blk * 2 * m, row, out)
    return out.reshape(n, w)


def _gla_kernel(q_ref, k_ref, v_ref, g_ref, low_ref, cwq_ref, cwk_ref, cwv_ref, cbq_ref, cbk_ref, cbv_ref,
                gw_ref, gb_ref, ng_ref, mask_ref, y_ref, qs, ks, vs, yf, yb, st_scr, *, n_chunks):
    n_rows = n_chunks * LC
    qs[...] = (_conv_silu(q_ref, cwq_ref, cbq_ref, n_rows) * (B_DK ** -0.5)).astype(BF16)
    ks[...] = _conv_silu(k_ref, cwk_ref, cbk_ref, n_rows).astype(BF16)
    vs[...] = _conv_silu(v_ref, cwv_ref, cbv_ref, n_rows).astype(BF16)
    st_scr[...] = jnp.zeros_like(st_scr)
    tril, triu = _tri_masks()
    tril_b, triu_b = tril.astype(BF16), triu.astype(BF16)
    eye = jnp.logical_and(tril, triu).astype(F32)

    def step(cf, cb):
        dirs = (0, 1)
        fwd = (True, False)
        rows = [pl.ds(pl.multiple_of(c * LC, LC), LC) for c in (cf, cb)]
        st = [st_scr[d] for d in dirs]
        pre = [jnp.dot(low_ref[rows[d], :], gw_ref[d], precision=lax.Precision.HIGHEST,
                       preferred_element_type=F32) + gb_ref[d:d + 1, :] for d in dirs]
        lg = [_log_sigmoid(pre[d]) * (1.0 / B_TAU) for d in dirs]
        b = [_dot_exact_lhs01(tril_b if fwd[d] else triu_b, lg[d]) for d in dirs]
        q = [qs[rows[d], :].astype(F32) for d in dirs]
        k = [ks[rows[d], :].astype(F32) for d in dirs]
        v = [vs[rows[d], :] for d in dirs]
        a = [eye * jnp.sum(q[d] * k[d], axis=-1, keepdims=True) for d in dirs]
        for lev in range(GLA_LEVELS):
            for d in dirs:
                decay = jnp.exp(-jnp.abs(b[d] - _seg_ref(b[d], 1 << lev, fwd[d])))
                qt = (q[d] * decay).astype(BF16)
                kt = (k[d] * decay).astype(BF16)
                a[d] = a[d] + mask_ref[d * GLA_LEVELS + lev] * _dot_nt(qt, kt)
        intra = [_dot(a[d].astype(BF16), v[d]) for d in dirs]
        inter = [_dot_nt((q[d] * jnp.exp(b[d])).astype(BF16), st[d].astype(BF16)) for d in dirs]
        yf[rows[0], :] = intra[0] + inter[0]
        yb[rows[1], :] = intra[1] + inter[1]
        b_end = [b[0][LC - 1:LC, :], b[1][0:1, :]]
        upd = [_dot_tn(v[d], (k[d] * jnp.exp(b_end[d] - b[d])).astype(BF16)) for d in dirs]
        for d in dirs:
            st_scr[d] = st[d] * jnp.exp(b_end[d]) + upd[d]
    _scan_schedule(n_chunks, step)

    y = yf[...] + yb[...]
    yn = y * lax.rsqrt(jnp.mean(y * y, axis=-1, keepdims=True) + EPS)
    y_ref[...] = (yn * ng_ref[...] * _silu(g_ref[...].astype(F32))).astype(y_ref.dtype)


def gla_scan(p, low, conv_w, conv_b, gate_w, gate_b, norm_g, n_batch, n_chunks):
    rows = n_chunks * LC
    kq, kv, kg = B_HEADS, (2 * B_HEADS * B_DK) // B_DV, (2 * B_HEADS * B_DK + B_HEADS * B_DV) // B_DV
    masks = jnp.asarray(_gla_level_masks(), F32)
    cmap = lambda b, h: (0, 0, 0)
    return pl.pallas_call(
        functools.partial(_gla_kernel, n_chunks=n_chunks),
        grid=(n_batch, B_HEADS),
        in_specs=[pl.BlockSpec((rows, B_DK), lambda b, h: (b, h)),
                  pl.BlockSpec((rows, B_DK), lambda b, h: (b, kq + h)),
                  pl.BlockSpec((rows, B_DV), lambda b, h: (b, kv + h)),
                  pl.BlockSpec((rows, B_DV), lambda b, h: (b, kg + h)),
                  pl.BlockSpec((rows, 128), lambda b, h: (b, 0)),
                  pl.BlockSpec((3, B_DK), lambda b, h: (0, h)),
                  pl.BlockSpec((3, B_DK), lambda b, h: (0, kq + h)),
                  pl.BlockSpec((3, B_DV), lambda b, h: (0, kv + h)),
                  pl.BlockSpec((1, B_DK), lambda b, h: (0, h)),
                  pl.BlockSpec((1, B_DK), lambda b, h: (0, kq + h)),
                  pl.BlockSpec((1, B_DV), lambda b, h: (0, kv + h)),
                  pl.BlockSpec((2, 128, B_DK), lambda b, h: (0, 0, h)),
                  pl.BlockSpec((2, B_DK), lambda b, h: (0, h)),
                  pl.BlockSpec((1, B_DV), lambda b, h: (0, h)),
                  pl.BlockSpec((2 * GLA_LEVELS, LC, LC), cmap)],
        out_specs=pl.BlockSpec((rows, B_DV), lambda b, h: (b, h)),
        out_shape=jax.ShapeDtypeStruct((n_batch * rows, B_HEADS * B_DV), BF16),
        scratch_shapes=[pltpu.VMEM((rows, B_DK), BF16), pltpu.VMEM((rows, B_DK), BF16),
                        pltpu.VMEM((rows, B_DV), BF16),
                        pltpu.VMEM((rows, B_DV), F32), pltpu.VMEM((rows, B_DV), F32),
                        pltpu.VMEM((2, B_DV, B_DK), F32)],
        compiler_params=_cparams("parallel", "parallel"),
        name="gla_scan",
    )(p, p, p, p, low, conv_w, conv_w, conv_w, conv_b, conv_b, conv_b, gate_w, gate_b, norm_g, masks)


FB = 256


def _dft_mats(n_len):
    n = 2 * n_len
    nfb = n_len // FB
    f = np.arange(n_len)[:, None]
    t = np.arange(n_len)[None, :]
    ang = 2.0 * np.pi * ((f * t) % n) / n
    alt = (-1.0) ** np.arange(n_len)
    cf, sf = np.cos(ang), -np.sin(ang)
    sf[0, :] = alt
    fwd = np.stack([cf.reshape(nfb, FB, n_len), sf.reshape(nfb, FB, n_len)], axis=1).reshape(n, n_len)
    ci, si = (2.0 / n) * np.cos(ang.T), -(2.0 / n) * np.sin(ang.T)
    ci[:, 0] = 1.0 / n
    si[:, 0] = alt / n
    inv = np.stack([ci.reshape(n_len, nfb, FB), si.reshape(n_len, nfb, FB)], axis=2).reshape(n_len, n)
    return fwd.astype(np.float32), inv.astype(np.float32)


def _hy_pre_kernel(x0_ref, x1_ref, v_ref, cw0, cw1, cw2, cb0, cb1, cb2, zc_ref, zl_ref, x0c_ref, x0l_ref,
                   *, n_rows):
    x0 = _conv3(x0_ref, cw0, cb0, n_rows)
    z = _conv3(x1_ref, cw1, cb1, n_rows) * _conv3(v_ref, cw2, cb2, n_rows)
    zc_ref[...] = z[:LC].astype(BF16)
    zl_ref[...] = z[LC:].astype(BF16)
    x0c_ref[...] = x0[:LC].astype(BF16)
    x0l_ref[...] = x0[LC:].astype(BF16)


def hyena_pre(p, conv_w, conv_b, n_batch, n_chunks):
    rows = n_chunks * LC
    d = p.shape[1] // 3
    w = 256
    nj = d // w
    s_len = rows - LC
    spec = lambda k: pl.BlockSpec((rows, w), lambda b, j: (b, k * nj + j))
    cws = lambda k: pl.BlockSpec((3, w), lambda b, j: (0, k * nj + j))
    cbs = lambda k: pl.BlockSpec((1, w), lambda b, j: (0, k * nj + j))
    oc = pl.BlockSpec((LC, w), lambda b, j: (b, j))
    ol = pl.BlockSpec((s_len, w), lambda b, j: (b, j))
    return pl.pallas_call(
        functools.partial(_hy_pre_kernel, n_rows=rows),
        grid=(n_batch, nj),
        in_specs=[spec(0), spec(1), spec(2), cws(0), cws(1), cws(2), cbs(0), cbs(1), cbs(2)],
        out_specs=[oc, ol, oc, ol],
        out_shape=[jax.ShapeDtypeStruct((n_batch * LC, d), BF16), jax.ShapeDtypeStruct((n_batch * s_len, d), BF16),
                   jax.ShapeDtypeStruct((n_batch * LC, d), BF16), jax.ShapeDtypeStruct((n_batch * s_len, d), BF16)],
        compiler_params=_cparams("parallel", "parallel"),
        name="hyena_pre",
    )(p, p, p, conv_w, conv_w, conv_w, conv_b, conv_b, conv_b)


def _filt_dft_kernel(f_ref, x_ref, o_ref):
    acc = _dot(f_ref[...], x_ref[0])
    half = acc.shape[1] // 2
    o_ref[0] = acc[:, :half] + acc[:, half:]


def filter_spectrum(fwd, hsum, hdiff):
    n_len, d = hsum.shape

    def hilo(a):
        hi = a.astype(BF16)
        return jnp.concatenate([hi, (a - hi.astype(F32)).astype(BF16)], axis=1)
    xs = jnp.stack([hilo(hsum), hilo(hdiff)])
    return pl.pallas_call(
        _filt_dft_kernel,
        grid=(2, n_len // FB),
        in_specs=[pl.BlockSpec((FB, n_len), lambda w, m: (2 * m + w, 0)),
                  pl.BlockSpec((1, n_len, 2 * d), lambda w, m: (w, 0, 0))],
        out_specs=pl.BlockSpec((1, FB, d), lambda w, m: (w, m, 0)),
        out_shape=jax.ShapeDtypeStruct((2, n_len, d), F32),
        compiler_params=_cparams("parallel", "parallel"),
        name="filter_spectrum",
    )(fwd, xs)


def _dft_fwd_kernel(f_ref, z_ref, kr_ref, ki_ref, y_ref):
    acc = _dot(f_ref[...], z_ref[...])
    zr, zi = acc[:FB], acc[FB:]
    kr, ki = kr_ref[...], ki_ref[...]
    packed = jnp.logical_and(pl.program_id(1) == 0, lax.broadcasted_iota(I32, (FB, 1), 0) == 0)
    y_ref[:FB] = (zr * kr - jnp.where(packed, 0.0, zi * ki)).astype(BF16)
    y_ref[FB:] = jnp.where(packed, zi * ki, zr * ki + zi * kr).astype(BF16)


def dft_forward(fwd, z, kr, ki, n_batch):
    n_len, d = kr.shape
    nfb = n_len // FB
    return pl.pallas_call(
        _dft_fwd_kernel,
        grid=(n_batch, nfb),
        in_specs=[pl.BlockSpec((2 * FB, n_len), lambda b, m: (m, 0)),
                  pl.BlockSpec((n_len, d), lambda b, m: (b, 0)),
                  pl.BlockSpec((FB, d), lambda b, m: (m, 0)),
                  pl.BlockSpec((FB, d), lambda b, m: (m, 0))],
        out_specs=pl.BlockSpec((2 * FB, d), lambda b, m: (b * nfb + m, 0)),
        out_shape=jax.ShapeDtypeStruct((n_batch * 2 * n_len, d), BF16),
        compiler_params=_cparams("parallel", "arbitrary"),
        name="dft_forward",
    )(fwd, z, kr, ki)


def _dft_inv_kernel(ic_ref, yc_ref, zc_ref, x0c_ref, il_ref, yl_ref, zl_ref, x0l_ref, skip_ref, o_ref):
    def emit(i_ref, y_ref, z_ref, x0_ref):
        conv = _dot(i_ref[...], y_ref[...])
        o_ref[...] = (x0_ref[...].astype(F32) * (conv + z_ref[...].astype(F32) * skip_ref[...])).astype(o_ref.dtype)

    @pl.when(pl.program_id(1) == 0)
    def _():
        emit(ic_ref, yc_ref, zc_ref, x0c_ref)

    @pl.when(pl.program_id(1) > 0)
    def _():
        emit(il_ref, yl_ref, zl_ref, x0l_ref)


def dft_inverse(ctx_parts, lat_parts, skip, n_batch, tiles_per_batch):
    inv_c, yf_c, z_c, x0_c = ctx_parts
    inv_l, yf_l, z_l, x0_l = lat_parts
    d = z_l.shape[1]
    n_l = z_l.shape[0] // n_batch
    nt = n_l // TM
    lt = lambda m: jnp.maximum(m - 1, 0)
    return pl.pallas_call(
        _dft_inv_kernel,
        grid=(n_batch, tiles_per_batch),
        in_specs=[pl.BlockSpec((TM, 2 * LC), lambda b, m: (0, 0)),
                  pl.BlockSpec((2 * LC, d), lambda b, m: (b, 0)),
                  pl.BlockSpec((TM, d), lambda b, m: (b, 0)),
                  pl.BlockSpec((TM, d), lambda b, m: (b, 0)),
                  pl.BlockSpec((TM, 2 * n_l), lambda b, m: (lt(m), 0)),
                  pl.BlockSpec((2 * n_l, d), lambda b, m: (b, 0)),
                  pl.BlockSpec((TM, d), lambda b, m: (b * nt + lt(m), 0)),
                  pl.BlockSpec((TM, d), lambda b, m: (b * nt + lt(m), 0)),
                  pl.BlockSpec((1, d), lambda b, m: (0, 0))],
        out_specs=pl.BlockSpec((TM, d), lambda b, m: (b * tiles_per_batch + m, 0)),
        out_shape=jax.ShapeDtypeStruct((n_batch * tiles_per_batch * TM, d), BF16),
        compiler_params=_cparams("parallel", "arbitrary"),
        name="dft_inverse",
    )(inv_c, yf_c, z_c, x0_c, inv_l, yf_l, z_l, x0_l, skip.reshape(1, d))


def _hyena_filters(n_len, w1, b1, w_mid, b_mid, w_out, freq):
    hp = lax.Precision.HIGHEST
    d = w_out.shape[1] // 2
    n_bands = (w1.shape[0] - 1) // 2
    t = jnp.linspace(0.0, 1.0, n_len, dtype=F32)[:, None]
    pos = jnp.arange(n_len, dtype=F32)[:, None]
    bands = jnp.linspace(1e-4, n_bands - 1, n_bands, dtype=F32)[None, :]
    ang = (2.0 * math.pi / n_len) * pos * bands
    feats = jnp.concatenate([t, jnp.cos(ang), -jnp.sin(ang)], -1)
    h = jnp.sin(freq * (jnp.dot(feats, w1, precision=hp) + b1))
    for m in range(w_mid.shape[0]):
        h = jnp.sin(freq * (jnp.dot(h, w_mid[m], precision=hp) + b_mid[m]))
    h = jnp.dot(h, w_out, precision=hp).reshape(n_len, 2, d)
    deltas = jnp.abs(jnp.linspace(math.log(1e-2) / 1.5, math.log(1e-2) / 0.3, d, dtype=F32))
    h = h * jnp.exp(-t * deltas)[:, None, :]
    hf, hb = h[:, 0], h[:, 1]
    l1 = jnp.sum(jnp.abs(hf), 0) + jnp.sum(jnp.abs(hb[1:]), 0)
    return hf / l1, hb / l1


def hyena_mix(p, conv_w, conv_b, filt, skip, n_batch, n_chunks):
    rows = n_chunks * LC
    zc, zl, x0c, x0l = hyena_pre(p, conv_w, conv_b, n_batch, n_chunks)
    parts = []
    for z, x0 in ((zc, x0c), (zl, x0l)):
        n_len = z.shape[0] // n_batch
        fwd_np, inv_np = _dft_mats(n_len)
        fwd, inv = jnp.asarray(fwd_np, F32).astype(BF16), jnp.asarray(inv_np, F32).astype(BF16)
        hf, hb = _hyena_filters(n_len, *filt)
        hb0 = hb.at[0].set(0.0)
        hsum, hdiff = hf + hb0, hf - hb0
        spec = filter_spectrum(fwd, hsum, hdiff)
        alt = jnp.asarray((-1.0) ** np.arange(n_len), F32)[:, None]
        kr, ki = spec[0], spec[1].at[0].set(jnp.sum(alt * hsum, axis=0))
        parts.append((inv, dft_forward(fwd, z, kr, ki, n_batch), z, x0))
    return dft_inverse(parts[0], parts[1], skip, n_batch, n_chunks)


P_DTYPE = BF16
N_MAIN = 3072
N_EXTRA = 128


def _pad_cols(w, n):
    return jnp.pad(w, ((0, 0), (0, n - w.shape[1])))


def _to_scan_order(a, n_batch, ctx_len):
    w = a.shape[1]
    a3 = a.reshape(n_batch, -1, w)
    lat = a3[:, ctx_len:]
    s_len = lat.shape[1]
    lat = lat.reshape(n_batch, s_len // GRID_W, GRID_W, w).transpose(0, 2, 1, 3).reshape(n_batch, s_len, w)
    return jnp.concatenate([a3[:, :ctx_len], lat], axis=1).reshape(-1, w)


def _from_scan_order(a, n_batch, ctx_len):
    w = a.shape[1]
    a3 = a.reshape(n_batch, -1, w)
    lat = a3[:, ctx_len:]
    s_len = lat.shape[1]
    lat = lat.reshape(n_batch, GRID_W, s_len // GRID_W, w).transpose(0, 2, 1, 3).reshape(n_batch, s_len, w)
    return jnp.concatenate([a3[:, :ctx_len], lat], axis=1).reshape(-1, w)


def kernel(x, c, ctx, c_ctx, ada_w, ada_b, norm1_g, norm2_g, final_g, ml_w_in, ml_b_gate, ml_conv_w, ml_conv_b, ml_norm_g, ml_w_out, gla_w_in, gla_conv_w, gla_conv_b, gla_gate_w2, gla_gate_b, gla_norm_g, gla_w_out, hy_w_in, hy_b_in, hy_conv_w, hy_conv_b, hy_filt_w1, hy_filt_b1, hy_filt_w_mid, hy_filt_b_mid, hy_filt_w_out, hy_filt_freq, hy_skip, hy_w_out, hy_b_out, router_w, router_b, exp_w1, exp_w3, exp_w2):
    n_batch, s_len, d = x.shape
    ctx_len = ctx.shape[1]
    depth = ada_w.shape[0]
    assert ctx_len == LC == TM and s_len % LC == 0 and s_len % GRID_W == 0
    n_chunks = (ctx_len + s_len) // LC
    t_all = n_batch * n_chunks * LC
    lat_tiles = s_len // TM

    mod_rows = jnp.zeros((16, d), F32).at[0].set(c_ctx).at[1:1 + n_batch].set(c)
    mods = ada_mods(mod_rows, ada_w, ada_b).reshape(depth, 16, 6, d)
    xs = jnp.concatenate([ctx, x], axis=1).reshape(t_all, d)
    router_wt = router_w.T
    zeros_d = jnp.zeros((d,), F32)
    out = None

    for i in range(depth):
        last = i == depth - 1
        kind, j = i % 3, i // 3
        if kind == 0:
            w_in = _pad_cols(ml_w_in[j], N_MAIN + N_EXTRA).astype(BF16)
            bias = jnp.zeros((N_MAIN + N_EXTRA,), F32).at[N_MAIN:N_MAIN + 4 * A_HEADS].set(ml_b_gate[j])
            p, pg = in_proj(xs, mods[i], norm1_g[i], w_in, bias, N_MAIN, n_chunks, P_DTYPE)
            g4 = pg[:, :4 * A_HEADS].reshape(t_all, 4, A_HEADS // 2, 2)
            g_cols = g4.transpose(2, 0, 1, 3).reshape(A_HEADS // 2, t_all, 8)
            g_rows = g_cols.reshape(A_HEADS // 2, n_batch, n_chunks, LC, 8).transpose(0, 1, 2, 4, 3)
            y = mlstm_scan(p, g_rows, ml_conv_w[j], ml_conv_b[j].reshape(1, -1),
                           ml_norm_g[j].reshape(1, -1), n_batch, n_chunks)
            w_out, b_out = ml_w_out[j], zeros_d
        elif kind == 1:
            w_in = _pad_cols(gla_w_in[j], N_MAIN + N_EXTRA).astype(BF16)
            bias = jnp.zeros((N_MAIN + N_EXTRA,), F32)
            p, pg = in_proj(xs, mods[i], norm1_g[i], w_in, bias, N_MAIN, n_chunks, P_DTYPE)
            p = _to_scan_order(p, n_batch, ctx_len)
            pg = _to_scan_order(pg, n_batch, ctx_len)
            gate_w = jnp.zeros((2, 128, B_HEADS * B_DK), F32)
            gate_w = gate_w.at[0, :B_RANK].set(gla_gate_w2[j, 0]).at[1, B_RANK:2 * B_RANK].set(gla_gate_w2[j, 1])
            y = gla_scan(p, pg, gla_conv_w[j], gla_conv_b[j].reshape(1, -1), gate_w, gla_gate_b[j],
                         gla_norm_g[j].reshape(1, -1), n_batch, n_chunks)
            y = _from_scan_order(y, n_batch, ctx_len)
            w_out, b_out = gla_w_out[j], zeros_d
        else:
            (p,) = in_proj(xs, mods[i], norm1_g[i], hy_w_in[j].astype(BF16), hy_b_in[j], N_MAIN, n_chunks, P_DTYPE)
            filt = (hy_filt_w1[j], hy_filt_b1[j], hy_filt_w_mid[j], hy_filt_b_mid[j], hy_filt_w_out[j],
                    hy_filt_freq[j])
            y = hyena_mix(p, hy_conv_w[j], hy_conv_b[j].reshape(1, -1), filt, hy_skip[j], n_batch, n_chunks)
            w_out, b_out = hy_w_out[j], hy_b_out[j]

        x2, h2, ids, gates, ranks, counts = out_route(
            xs, y, mods[i], w_out.astype(BF16), b_out, norm2_g[i], router_wt, router_b, n_chunks, 1, last)
        n_tok = x2.shape[0]
        dest_tiles, n_slots, block_exp, n_used, pad_end = moe_plan(ids, ranks, counts, n_tok)
        xb = moe_dispatch(h2, dest_tiles, pad_end, n_used, n_slots)
        yb = moe_experts(xb, block_exp, n_used, exp_w1, exp_w3, exp_w2, i)
        if last:
            out = moe_combine(x2, yb, dest_tiles, gates.T, mods[i], lambda t: 1 + t // lat_tiles, final_g)
        else:
            xs = moe_combine(x2, yb, dest_tiles, gates.T, mods[i],
                             lambda t: jnp.where(t % n_chunks == 0, 0, 1 + t // n_chunks))
    return out.reshape(n_batch, s_len, d)
```

```python
import functools
import math

import numpy as np
import jax
import jax.numpy as jnp
from jax import lax
from jax.experimental import pallas as pl
from jax.experimental.pallas import tpu as pltpu

F32 = jnp.float32
BF16 = jnp.bfloat16
I32 = jnp.int32

EPS = 1e-6
TM = 256
LC = 256
MOE_BLOCK = 512
N_EXPERTS = 16
N_GROUPS = 4
EPG = N_EXPERTS // N_GROUPS
A_HEADS, A_DK, A_DV = 8, 64, 128
B_HEADS, B_DK, B_DV = 4, 128, 256
B_RANK = 16
B_TAU = 16.0
GRID_W = 64
VMEM_LIMIT_V7X = 56 * 1024 * 1024


def _cparams(*sem):
    return pltpu.CompilerParams(dimension_semantics=sem, vmem_limit_bytes=VMEM_LIMIT_V7X)


def _silu(x):
    return x * (1.0 / (1.0 + jnp.exp(-x)))


def _sigmoid(x):
    return 1.0 / (1.0 + jnp.exp(-x))


def _log_sigmoid(x):
    return jnp.minimum(x, 0.0) - jnp.log(1.0 + jnp.exp(-jnp.abs(x)))


def _split3(x):
    hi = x.astype(BF16)
    r1 = x - hi.astype(F32)
    mid = r1.astype(BF16)
    lo = (r1 - mid.astype(F32)).astype(BF16)
    return hi, mid, lo


def _dot(a, b):
    return jnp.dot(a, b, preferred_element_type=F32)


def _dot_nt(a, b):
    return lax.dot_general(a, b, (((1,), (1,)), ((), ())), preferred_element_type=F32)


def _dot_tn(a, b):
    return lax.dot_general(a, b, (((0,), (0,)), ((), ())), preferred_element_type=F32)


LANES = 128
SUBLANES = 8


def _tok_tiles_store(ref, base, x):
    n, d = x.shape
    per = d // LANES
    for c in range(per):
        ref[pl.ds(base + c, n, stride=per), :] = x[:, c * LANES:(c + 1) * LANES]


def _tok_tiles_load(ref, base, n, d):
    per = d // LANES
    return jnp.concatenate([ref[pl.ds(base + c, n, stride=per), :] for c in range(per)], axis=1)


def _dot_exact_lhs01(m01, x):
    hi, mid, lo = _split3(x)
    return (_dot(m01, hi) + _dot(m01, mid)) + _dot(m01, lo)


def _dot_exact_rhs01(x, m01):
    hi, mid, lo = _split3(x)
    return (_dot(hi, m01) + _dot(mid, m01)) + _dot(lo, m01)


def _ada_kernel(s_ref, w_ref, b_ref, o_ref):
    s = _silu(s_ref[...]).astype(BF16)
    o_ref[0] = _dot(s, w_ref[0].astype(BF16)) + b_ref[0]


def ada_mods(rows, ada_w, ada_b):
    depth, d, n6 = ada_w.shape
    nr = rows.shape[0]
    tn = 1536
    return pl.pallas_call(
        _ada_kernel,
        grid=(depth, n6 // tn),
        in_specs=[pl.BlockSpec((nr, d), lambda l, j: (0, 0)),
                  pl.BlockSpec((1, d, tn), lambda l, j: (l, 0, j)),
                  pl.BlockSpec((1, 1, tn), lambda l, j: (l, 0, j))],
        out_specs=pl.BlockSpec((1, nr, tn), lambda l, j: (l, 0, j)),
        out_shape=jax.ShapeDtypeStruct((depth, nr, n6), F32),
        compiler_params=_cparams("parallel", "parallel"),
        name="ada_mods",
    )(rows, ada_w, ada_b.reshape(depth, 1, n6))


def _norm_mod(x, g, shift, scale):
    y = x * lax.rsqrt(jnp.mean(x * x, axis=-1, keepdims=True) + EPS) * g
    return y * (1.0 + scale) + shift


def _in_proj_kernel(x_ref, mod_ref, g_ref, w_ref, b_ref, p_ref, *rest, n_main):
    h = _norm_mod(x_ref[...], g_ref[...], mod_ref[0, 0:1, :], mod_ref[0, 1:2, :]).astype(BF16)
    p_ref[...] = (_dot(h, w_ref[:, :n_main]) + b_ref[:, :n_main]).astype(p_ref.dtype)
    if rest:
        rest[0][...] = _dot(h, w_ref[:, n_main:]) + b_ref[:, n_main:]


def in_proj(x, mods_l, g, w, b, n_main, tiles_per_batch, p_dtype):
    t, d = x.shape
    n = w.shape[1]
    n_extra = n - n_main
    mod_row = lambda i: jnp.where(i % tiles_per_batch == 0, 0, 1 + i // tiles_per_batch)
    out_shape = [jax.ShapeDtypeStruct((t, n_main), p_dtype)]
    out_specs = [pl.BlockSpec((TM, n_main), lambda i: (i, 0))]
    if n_extra:
        out_shape.append(jax.ShapeDtypeStruct((t, n_extra), F32))
        out_specs.append(pl.BlockSpec((TM, n_extra), lambda i: (i, 0)))
    return pl.pallas_call(
        functools.partial(_in_proj_kernel, n_main=n_main),
        grid=(t // TM,),
        in_specs=[pl.BlockSpec((TM, d), lambda i: (i, 0)),
                  pl.BlockSpec((1, 6, d), lambda i: (mod_row(i), 0, 0)),
                  pl.BlockSpec((1, d), lambda i: (0, 0)),
                  pl.BlockSpec((d, n), lambda i: (0, 0)),
                  pl.BlockSpec((1, n), lambda i: (0, 0))],
        out_specs=out_specs,
        out_shape=out_shape,
        compiler_params=_cparams("parallel"),
        name="in_proj",
    )(x, mods_l, g.reshape(1, d), w, b.reshape(1, n))


def _first_argmax(vals):
    best, idx = vals[0], jnp.zeros(vals[0].shape, I32)
    for i in range(1, len(vals)):
        take = vals[i] > best
        best = jnp.where(take, vals[i], best)
        idx = jnp.where(take, i, idx)
    return best, idx


def _select_row(rows, idx):
    out = rows[0]
    for i in range(1, len(rows)):
        out = jnp.where(idx == i, rows[i], out)
    return out


def _route(scores, sel):
    neg = jnp.float32(-jnp.inf)
    srow = [sel[e:e + 1, :] for e in range(N_EXPERTS)]
    crow = [scores[e:e + 1, :] for e in range(N_EXPERTS)]
    gscore = []
    for g in range(N_GROUPS):
        a = srow[g * EPG:(g + 1) * EPG]
        m1, i1 = _first_argmax(a)
        m2, _ = _first_argmax([jnp.where(i1 == i, neg, a[i]) for i in range(EPG)])
        gscore.append(m1 + m2)
    _, grp = _first_argmax(gscore)
    in_sel = [_select_row([srow[g * EPG + i] for g in range(N_GROUPS)], grp) for i in range(EPG)]
    in_sc = [_select_row([crow[g * EPG + i] for g in range(N_GROUPS)], grp) for i in range(EPG)]
    _, l1 = _first_argmax(in_sel)
    _, l2 = _first_argmax([jnp.where(l1 == i, neg, in_sel[i]) for i in range(EPG)])
    g1 = _select_row(in_sc, l1)
    g2 = _select_row(in_sc, l2)
    tot = g1 + g2
    ids = jnp.concatenate([grp * EPG + l1, grp * EPG + l2], axis=0)
    gates = jnp.concatenate([g1 / tot, g2 / tot], axis=0)
    return ids, gates


ROUTE_SUB = 4


def _out_route_kernel(*refs, mod_row):
    x_refs, y_refs = refs[:ROUTE_SUB], refs[ROUTE_SUB:2 * ROUTE_SUB]
    (mod_ref, w_ref, b_ref, g_ref, rw_ref, rb_ref,
     xo_ref, h_ref, ids_ref, gates_ref, rank_ref, cnt_ref, carry_ref) = refs[2 * ROUTE_SUB:]
    i = pl.program_id(0)

    @pl.when(i == 0)
    def _():
        carry_ref[...] = jnp.zeros_like(carry_ref)

    halves = tuple(range(ROUTE_SUB))
    rows = [slice(k * TM, (k + 1) * TM) for k in halves]
    mod = [mod_ref[mod_row(ROUTE_SUB * i + k)] for k in halves]
    proj = [_dot(y_ref[...], w_ref[...]) for y_ref in y_refs]
    xn = [x_ref[...] + mod[k][2:3, :] * (proj[k] + b_ref[...]) for k, x_ref in enumerate(x_refs)]
    h = [_norm_mod(xn[k], g_ref[...], mod[k][3:4, :], mod[k][4:5, :]) for k in halves]
    rw = rw_ref[...]
    rw_hi = rw.astype(BF16)
    rw_2 = jnp.concatenate([rw_hi, (rw - rw_hi.astype(F32)).astype(BF16)], axis=0)
    h_hi = [h[k].astype(BF16) for k in halves]
    h_lo = [(h[k] - h_hi[k].astype(F32)).astype(BF16) for k in halves]
    l_1 = [_dot_nt(rw_2, h_hi[k]) for k in halves]
    l_2 = [_dot_nt(rw_hi, h_lo[k]) for k in halves]
    logits = [(l_1[k][:N_EXPERTS] + l_1[k][N_EXPERTS:]) + l_2[k] for k in halves]
    for k in halves:
        xo_ref[rows[k], :] = xn[k]
        _tok_tiles_store(h_ref, k * TM * SUBLANES, h[k])
    scores = [_sigmoid(logits[k]) for k in halves]
    routes = [_route(scores[k], scores[k] + rb_ref[:, 0:1]) for k in halves]
    erow = lax.broadcasted_iota(I32, (N_EXPERTS, TM), 0)
    before = (lax.broadcasted_iota(I32, (TM, TM), 0) < lax.broadcasted_iota(I32, (TM, TM), 1)).astype(BF16)
    oh = [[(erow == routes[k][0][j:j + 1, :]).astype(F32) for j in range(2)] for k in halves]
    pre = [[_dot(oh[k][j].astype(BF16), before) for j in range(2)] for k in halves]
    tot = [[jnp.sum(oh[k][j], axis=1, keepdims=True) for j in range(2)] for k in halves]

    carry = carry_ref[:, 0:1]
    for k in halves:
        ids_ref[:, rows[k]] = routes[k][0]
        gates_ref[:, rows[k]] = routes[k][1]
        r0 = jnp.sum(oh[k][0] * (carry + pre[k][0]), axis=0, keepdims=True)
        r1 = jnp.sum(oh[k][1] * (carry + tot[k][0] + pre[k][1]), axis=0, keepdims=True)
        rank_ref[:, rows[k]] = jnp.concatenate([r0, r1], axis=0).astype(I32)
        carry = carry + tot[k][0] + tot[k][1]
    carry_ref[...] = jnp.broadcast_to(carry, carry_ref.shape)
    cnt_ref[...] = jnp.broadcast_to(carry, cnt_ref.shape).astype(I32)


def out_route(x, y, mods_l, w_out, b_out, g2n, router_wt, router_b, tiles_per_batch, ctx_tiles, skip_ctx):
    t, d = x.shape
    nb = t // (tiles_per_batch * TM)
    if skip_ctx:
        per = tiles_per_batch - ctx_tiles
        src = lambda i: (i // per) * tiles_per_batch + ctx_tiles + i % per
        mod_row = lambda i: 1 + i // per
        n_tiles = nb * per
    else:
        src = lambda i: i
        mod_row = lambda i: jnp.where(i % tiles_per_batch < ctx_tiles, 0, 1 + i // tiles_per_batch)
        n_tiles = nb * tiles_per_batch
    ns = ROUTE_SUB
    assert n_tiles % ns == 0 and d == SUBLANES * LANES
    tr = n_tiles * TM
    tok = lambda i: (i, 0)
    row2 = lambda i: (0, i)
    tile_specs = [pl.BlockSpec((TM, d), functools.partial(lambda i, k: (src(ns * i + k), 0), k=k))
                  for k in range(ns)]
    outs = pl.pallas_call(
        functools.partial(_out_route_kernel, mod_row=mod_row),
        grid=(n_tiles // ns,),
        in_specs=tile_specs + tile_specs + [
            pl.BlockSpec(mods_l.shape, lambda i: (0, 0, 0)),
            pl.BlockSpec((d, d), lambda i: (0, 0)),
            pl.BlockSpec((1, d), lambda i: (0, 0)),
            pl.BlockSpec((1, d), lambda i: (0, 0)),
            pl.BlockSpec((N_EXPERTS, d), lambda i: (0, 0)),
            pl.BlockSpec((N_EXPERTS, 128), lambda i: (0, 0))],
        out_specs=[pl.BlockSpec((ns * TM, d), tok),
                   pl.BlockSpec((ns * TM * SUBLANES, LANES), tok),
                   pl.BlockSpec((2, ns * TM), row2),
                   pl.BlockSpec((2, ns * TM), row2),
                   pl.BlockSpec((2, ns * TM), row2),
                   pl.BlockSpec((N_EXPERTS, 128), lambda i: (0, 0))],
        out_shape=[jax.ShapeDtypeStruct((tr, d), F32),
                   jax.ShapeDtypeStruct((tr * SUBLANES, LANES), F32),
                   jax.ShapeDtypeStruct((2, tr), I32),
                   jax.ShapeDtypeStruct((2, tr), F32),
                   jax.ShapeDtypeStruct((2, tr), I32),
                   jax.ShapeDtypeStruct((N_EXPERTS, 128), I32)],
        scratch_shapes=[pltpu.VMEM((N_EXPERTS, 128), F32)],
        compiler_params=_cparams("arbitrary"),
        name="out_route",
    )(*([x] * ns + [y] * ns), mods_l, w_out, b_out.reshape(1, d), g2n.reshape(1, d), router_wt,
      jnp.broadcast_to(router_b.reshape(N_EXPERTS, 1), (N_EXPERTS, 128)))
    return outs


def _moe_kernel(bexp_ref, nused_ref, x_ref, w1_ref, w3_ref, w2_ref, o_ref, w1b, w3b, w2b):
    i = pl.program_id(0)
    prev = bexp_ref[jnp.maximum(i - 1, 0)]
    fresh = jnp.logical_or(i == 0, bexp_ref[i] != prev)
    used = i < nused_ref[0]

    @pl.when(jnp.logical_and(fresh, used))
    def _():
        w1b[...] = w1_ref[0].astype(BF16)
        w3b[...] = w3_ref[0].astype(BF16)
        w2b[...] = w2_ref[0].astype(BF16)

    @pl.when(used)
    def _():
        x = _tok_tiles_load(x_ref, 0, MOE_BLOCK, w1b.shape[0]).astype(BF16)
        a = _dot(x, w1b[...])
        b = _dot(x, w3b[...])
        hmid = (_silu(a) * b).astype(BF16)
        _tok_tiles_store(o_ref, 0, _dot(hmid, w2b[...]))

    @pl.when(jnp.logical_not(used))
    def _():
        o_ref[...] = jnp.zeros_like(o_ref)


def moe_experts(xb, block_exp, n_used, w1, w3, w2, layer):
    d, ff = w1.shape[2], w1.shape[3]
    per = d // LANES
    n_slots, dh = xb.shape[0] // per, LANES
    n_blocks = n_slots // MOE_BLOCK
    wmap = lambda i, be, nu: (layer, be[i], 0, 0)
    return pl.pallas_call(
        _moe_kernel,
        grid_spec=pltpu.PrefetchScalarGridSpec(
            num_scalar_prefetch=2,
            grid=(n_blocks,),
            in_specs=[pl.BlockSpec((MOE_BLOCK * per, dh), lambda i, be, nu: (i, 0)),
                      pl.BlockSpec((None, 1, d, ff), wmap),
                      pl.BlockSpec((None, 1, d, ff), wmap),
                      pl.BlockSpec((None, 1, ff, d), wmap)],
            out_specs=pl.BlockSpec((MOE_BLOCK * per, dh), lambda i, be, nu: (i, 0)),
            scratch_shapes=[pltpu.VMEM((d, ff), BF16), pltpu.VMEM((d, ff), BF16),
                            pltpu.VMEM((ff, d), BF16)]),
        out_shape=jax.ShapeDtypeStruct((n_slots * per, dh), F32),
        compiler_params=_cparams("arbitrary"),
        name="moe_experts",
    )(block_exp, n_used, xb, w1, w3, w2)


def moe_plan(ids, ranks, counts, n_tokens):
    n_assign = 2 * n_tokens
    n_blocks = n_assign // MOE_BLOCK + N_EXPERTS
    cnt = counts[:, 0]
    padded = (cnt + MOE_BLOCK - 1) // MOE_BLOCK * MOE_BLOCK
    pad_end = jnp.cumsum(padded)
    pad_start = pad_end - padded
    start_of = sum(jnp.where(ids == e, pad_start[e], 0) for e in range(N_EXPERTS))
    dest = start_of + ranks
    blk_start = jnp.arange(n_blocks, dtype=I32) * MOE_BLOCK
    block_exp = jnp.minimum(jnp.sum(blk_start[:, None] >= pad_end[None, :], axis=1), N_EXPERTS - 1).astype(I32)
    n_used = (pad_end[-1] // MOE_BLOCK).astype(I32).reshape(1)
    dest_tiles = dest.reshape(2, n_tokens // TM, TM).transpose(1, 0, 2)
    return dest_tiles, n_blocks * MOE_BLOCK, block_exp, n_used, pad_end.astype(I32)


def _tile_of(idx):
    return pl.ds(pl.multiple_of(idx * SUBLANES, SUBLANES), SUBLANES)


def _token_copies(src_of, dst_of, dest_ref, sem):
    def body(r, carry):
        for k in range(2):
            pltpu.make_async_copy(src_of(k, r, dest_ref[0, k, r]), dst_of(k, r, dest_ref[0, k, r]), sem).start()
        return carry
    lax.fori_loop(0, TM, body, 0, unroll=8)


def _dispatch_kernel(pad_end_ref, nused_ref, dest_ref, h_ref, xb_ref, zeros, ring, sem, zsem, *, n_blocks):
    i = pl.program_id(0)
    n = pl.num_programs(0)
    blk_rows = MOE_BLOCK * SUBLANES

    def zero_block(blk):
        return pltpu.make_async_copy(zeros, xb_ref.at[pl.ds(pl.multiple_of(blk * blk_rows, blk_rows), blk_rows)],
                                     zsem)

    @pl.when(i == 0)
    def _():
        zeros[...] = jnp.zeros_like(zeros)
        fills = []
        for e in range(N_EXPERTS):
            end = pad_end_ref[e]
            start = pad_end_ref[e - 1] if e else 0
            fills.append((end > start, end // MOE_BLOCK - 1))
        for j in range(N_EXPERTS):
            fills.append((nused_ref[0] + j < n_blocks, nused_ref[0] + j))
        for cond, blk in fills:
            @pl.when(cond)
            def _():
                zero_block(blk).start()
        for cond, blk in fills:
            @pl.when(cond)
            def _():
                zero_block(blk).wait()

    def tile_wait(slot):
        for _ in range(2):
            pltpu.make_async_copy(h_ref, xb_ref.at[pl.ds(0, TM * SUBLANES)], sem.at[slot]).wait()

    slot = i % 2

    @pl.when(i >= 2)
    def _():
        tile_wait(slot)
    ring[slot] = h_ref[...]
    _token_copies(lambda k, r, s: ring.at[slot, _tile_of(r)], lambda k, r, s: xb_ref.at[_tile_of(s)], dest_ref,
                  sem.at[slot])

    @pl.when(i == n - 1)
    def _():
        @pl.when(n >= 2)
        def _():
            tile_wait(1 - slot)
        tile_wait(slot)


def moe_dispatch(h, dest_tiles, pad_end, n_used, n_slots):
    rows, dh = h.shape
    return pl.pallas_call(
        functools.partial(_dispatch_kernel, n_blocks=n_slots // MOE_BLOCK),
        grid_spec=pltpu.PrefetchScalarGridSpec(
            num_scalar_prefetch=2,
            grid=(rows // (TM * SUBLANES),),
            in_specs=[pl.BlockSpec((1, 2, TM), lambda i, pe, nu: (i, 0, 0), memory_space=pltpu.SMEM),
                      pl.BlockSpec((TM * SUBLANES, dh), lambda i, pe, nu: (i, 0))],
            out_specs=pl.BlockSpec(memory_space=pl.ANY),
            scratch_shapes=[pltpu.VMEM((MOE_BLOCK * SUBLANES, dh), h.dtype),
                            pltpu.VMEM((2, TM * SUBLANES, dh), h.dtype),
                            pltpu.SemaphoreType.DMA((2,)), pltpu.SemaphoreType.DMA(())]),
        out_shape=jax.ShapeDtypeStruct((n_slots * SUBLANES, dh), h.dtype),
        compiler_params=_cparams("arbitrary"),
        name="moe_dispatch",
    )(pad_end, n_used, dest_tiles, h)


def _combine_kernel(dcur_ref, dnext_ref, x_ref, yb_ref, gt_ref, mod_ref, *rest, final):
    buf, sem = rest[-2:]
    i = pl.program_id(0)
    n = pl.num_programs(0)

    def fetch(dest_ref, slot):
        _token_copies(lambda k, r, s: yb_ref.at[_tile_of(s)], lambda k, r, s: buf.at[slot, k, _tile_of(r)],
                      dest_ref, sem.at[slot])

    @pl.when(i == 0)
    def _():
        fetch(dcur_ref, 0)

    @pl.when(i + 1 < n)
    def _():
        fetch(dnext_ref, (i + 1) % 2)

    slot = i % 2
    for k in range(2):
        pltpu.make_async_copy(yb_ref.at[pl.ds(0, TM * SUBLANES)], buf.at[slot, k], sem.at[slot]).wait()
    d = x_ref.shape[1]
    moe = (gt_ref[:, 0:1] * _tok_tiles_load(buf.at[slot, 0], 0, TM, d)
           + gt_ref[:, 1:2] * _tok_tiles_load(buf.at[slot, 1], 0, TM, d))
    xn = x_ref[...] + mod_ref[0, 5:6, :] * moe
    if final:
        g_ref, o_ref = rest[:2]
        o_ref[...] = xn * lax.rsqrt(jnp.mean(xn * xn, axis=-1, keepdims=True) + EPS) * g_ref[...]
    else:
        rest[0][...] = xn


def moe_combine(x, yb, dest_tiles, gates_t, mods_l, mod_row, final_g=None):
    t, d = x.shape
    n_tiles = t // TM
    tok = lambda i: (i, 0)
    in_specs = [pl.BlockSpec((1, 2, TM), lambda i: (i, 0, 0), memory_space=pltpu.SMEM),
                pl.BlockSpec((1, 2, TM), lambda i: (jnp.minimum(i + 1, n_tiles - 1), 0, 0), memory_space=pltpu.SMEM),
                pl.BlockSpec((TM, d), tok),
                pl.BlockSpec(memory_space=pl.ANY),
                pl.BlockSpec((TM, 2), tok),
                pl.BlockSpec((1, 6, d), lambda i: (mod_row(i), 0, 0))]
    args = [dest_tiles, dest_tiles, x, yb, gates_t, mods_l]
    if final_g is not None:
        in_specs.append(pl.BlockSpec((1, d), lambda i: (0, 0)))
        args.append(final_g.reshape(1, d))
    return pl.pallas_call(
        functools.partial(_combine_kernel, final=final_g is not None),
        grid=(n_tiles,),
        in_specs=in_specs,
        out_specs=pl.BlockSpec((TM, d), tok),
        out_shape=jax.ShapeDtypeStruct((t, d), F32),
        scratch_shapes=[pltpu.VMEM((2, 2, TM * SUBLANES, LANES), F32), pltpu.SemaphoreType.DMA((2,))],
        compiler_params=_cparams("arbitrary"),
        name="moe_combine",
    )(*args)


def _conv3(x_ref, cw_ref, cb_ref, n_rows):
    x = x_ref[...].astype(F32)
    row = lax.broadcasted_iota(I32, (n_rows, 1), 0)
    first = jnp.logical_or(row == 0, row == LC)
    last = jnp.logical_or(row == LC - 1, row == n_rows - 1)
    prev = jnp.where(first, 0.0, pltpu.roll(x, 1, 0))
    nxt = jnp.where(last, 0.0, pltpu.roll(x, n_rows - 1, 0))
    return prev * cw_ref[0:1, :] + x * cw_ref[1:2, :] + nxt * cw_ref[2:3, :] + cb_ref[...]


def _conv_silu(x_ref, cw_ref, cb_ref, n_rows):
    return _silu(_conv3(x_ref, cw_ref, cb_ref, n_rows))


def _tri_masks():
    ri = lax.broadcasted_iota(I32, (LC, LC), 0)
    ci = lax.broadcasted_iota(I32, (LC, LC), 1)
    return ri >= ci, ri <= ci


def _scan_schedule(n_chunks, step):
    step(0, 0)

    def body(t, carry):
        step(t, n_chunks - t)
        return carry
    lax.fori_loop(1, n_chunks, body, 0)


def _rows16(rows):
    n = rows[0].shape[1]
    ri = lax.broadcasted_iota(I32, (16, n), 0)
    out = jnp.zeros((16, n), F32)
    for i, r in enumerate(rows):
        out = jnp.where(ri == i, jnp.broadcast_to(r, (16, n)), out)
    return out.astype(BF16)


def _mlstm_kernel(q_ref, k_ref, v_ref, o_ref, gt_ref, cwq_ref, cwk_ref, cbq_ref, cbk_ref, ng_ref,
                  y_ref, qs, ks, kst, vt, yft, ybt, ct_scr, m_scr, *, n_chunks):
    n_rows = n_chunks * LC
    qs[...] = (_conv_silu(q_ref, cwq_ref, cbq_ref, n_rows) * (A_DK ** -0.5)).astype(BF16)
    kf = _conv_silu(k_ref, cwk_ref, cbk_ref, n_rows)
    ks[...] = kf.astype(BF16)
    for c in range(n_chunks):
        kst[c] = kf[c * LC:(c + 1) * LC, :].T.astype(BF16)
        vt[c] = v_ref[c * LC:(c + 1) * LC, :].astype(F32).T.astype(BF16)
    ct_scr[...] = jnp.zeros_like(ct_scr)
    m_scr[...] = jnp.zeros_like(m_scr)
    tril, triu = _tri_masks()
    tril_b, triu_b = tril.astype(BF16), triu.astype(BF16)
    ones_rows = (lax.broadcasted_iota(I32, (A_DV, LC), 0) == 0).astype(BF16)
    one = jnp.ones((1, LC), F32)
    neg = jnp.float32(-jnp.inf)

    f32 = lambda t: t.astype(F32)

    def step(cf, cb):
        chains = [(cf, True, 0), (cf, True, 1), (cb, False, 0), (cb, False, 1)]
        m_prev = [m_scr[i, 0:1, 0:1] for i in range(4)]
        ct = [ct_scr[i] for i in range(4)]
        gts = {True: gt_ref[0, 0, cf], False: gt_ref[0, 0, cb]}
        css = {True: _dot_exact_rhs01(_log_sigmoid(gts[True]), triu_b),
               False: _dot_exact_rhs01(_log_sigmoid(gts[False]), tril_b)}
        rows = [pl.ds(pl.multiple_of(c * LC, LC), LC) for c, _, _ in chains]
        qh = [qs[rows[i], hh * A_DK:(hh + 1) * A_DK] for i, (_, _, hh) in enumerate(chains)]
        kh = [ks[rows[i], hh * A_DK:(hh + 1) * A_DK] for i, (_, _, hh) in enumerate(chains)]
        qk_t = [_dot_nt(kh[i], qh[i]) for i in range(4)]
        fcum, g, log_d = [], [], []
        for c, fwd, hh in chains:
            d = 0 if fwd else 1
            fc_row = css[fwd][4 * d + 2 + hh:4 * d + 3 + hh, :]
            g_row = gts[fwd][4 * d + hh:4 * d + hh + 1, :] - fc_row
            f_hi, f_mid, f_lo = _split3(fc_row)
            g_hi, g_mid, g_lo = _split3(g_row)
            f_slab = _rows16([f32(f_hi), f32(f_mid), f32(f_lo), one, one, one])
            g_slab = _rows16([one, one, one, f32(g_hi), f32(g_mid), f32(g_lo)])
            log_d.append(jnp.where(triu if fwd else tril, _dot_tn(g_slab, f_slab), neg))
            fcum.append(fc_row)
            g.append(g_row)
        li = [fcum[i] + m_prev[i] for i in range(4)]
        m_row = [jnp.maximum(li[i], jnp.max(log_d[i], axis=0, keepdims=True)) for i in range(4)]
        sm_t = [(qk_t[i] * jnp.exp(log_d[i] - m_row[i])).astype(BF16) for i in range(4)]
        vext_t = [jnp.concatenate([vt[c, hh * A_DV:(hh + 1) * A_DV, :], ones_rows], axis=0)
                  for c, _, hh in chains]
        intra = [_dot(vext_t[i], sm_t[i]) for i in range(4)]
        inter = [_dot_nt(ct[i].astype(BF16), qh[i]) for i in range(4)]
        kw_t, m_new = [], []
        for i, (c, fwd, hh) in enumerate(chains):
            num_t = intra[i] + jnp.exp(li[i] - m_row[i]) * inter[i]
            den = num_t[A_DV:A_DV + 1, :]
            (yft if fwd else ybt)[c, hh * A_DV:(hh + 1) * A_DV, :] = num_t[:A_DV, :] * (
                1.0 / jnp.maximum(jnp.abs(den), jnp.exp(-m_row[i])))
            b_end = fcum[i][:, LC - 1:LC] if fwd else fcum[i][:, 0:1]
            m_new.append(jnp.maximum(b_end + m_prev[i], jnp.max(b_end + g[i], axis=1, keepdims=True)))
            kw_t.append((kst[c, hh * A_DK:(hh + 1) * A_DK, :].astype(F32)
                         * jnp.exp(b_end + g[i] - m_new[i])).astype(BF16))
            ct[i] = jnp.exp(b_end + m_prev[i] - m_new[i]) * ct[i]
        upd = [_dot_nt(vext_t[i], kw_t[i]) for i in range(4)]
        for i in range(4):
            ct_scr[i] = ct[i] + upd[i]
            m_scr[i] = jnp.broadcast_to(m_new[i], m_scr.shape[1:])
    _scan_schedule(n_chunks, step)

    for hh in range(2):
        sl = slice(hh * A_DV, (hh + 1) * A_DV)
        for c in range(n_chunks):
            y_t = yft[c, sl, :] + ybt[c, sl, :]
            yn = (y_t * lax.rsqrt(jnp.mean(y_t * y_t, axis=0, keepdims=True) + EPS)).T
            rs = slice(c * LC, (c + 1) * LC)
            y_ref[rs, sl] = (yn * ng_ref[:, sl] * _sigmoid(o_ref[rs, sl].astype(F32))).astype(y_ref.dtype)


def mlstm_scan(p, g_rows, conv_w, conv_b, norm_g, n_batch, n_chunks):
    rows = n_chunks * LC
    n_hp = A_HEADS // 2
    qk_w = 2 * A_DK
    v_w = 2 * A_DV
    kq, kv, ko = (A_HEADS * A_DK) // qk_w, (2 * A_HEADS * A_DK) // v_w, (2 * A_HEADS * A_DK + A_HEADS * A_DV) // v_w
    return pl.pallas_call(
        functools.partial(_mlstm_kernel, n_chunks=n_chunks),
        grid=(n_batch, n_hp),
        in_specs=[pl.BlockSpec((rows, qk_w), lambda b, h: (b, h)),
                  pl.BlockSpec((rows, qk_w), lambda b, h: (b, kq + h)),
                  pl.BlockSpec((rows, v_w), lambda b, h: (b, kv + h)),
                  pl.BlockSpec((rows, v_w), lambda b, h: (b, ko + h)),
                  pl.BlockSpec((1, 1, n_chunks, 8, LC), lambda b, h: (h, b, 0, 0, 0)),
                  pl.BlockSpec((3, qk_w), lambda b, h: (0, h)),
                  pl.BlockSpec((3, qk_w), lambda b, h: (0, kq + h)),
                  pl.BlockSpec((1, qk_w), lambda b, h: (0, h)),
                  pl.BlockSpec((1, qk_w), lambda b, h: (0, kq + h)),
                  pl.BlockSpec((1, v_w), lambda b, h: (0, h))],
        out_specs=pl.BlockSpec((rows, v_w), lambda b, h: (b, h)),
        out_shape=jax.ShapeDtypeStruct((n_batch * rows, A_HEADS * A_DV), BF16),
        scratch_shapes=[pltpu.VMEM((rows, qk_w), BF16), pltpu.VMEM((rows, qk_w), BF16),
                        pltpu.VMEM((n_chunks, qk_w, LC), BF16), pltpu.VMEM((n_chunks, v_w, LC), BF16),
                        pltpu.VMEM((n_chunks, v_w, LC), F32), pltpu.VMEM((n_chunks, v_w, LC), F32),
                        pltpu.VMEM((4, 2 * A_DV, A_DK), F32), pltpu.VMEM((4, 8, 128), F32)],
        compiler_params=_cparams("parallel", "parallel"),
        name="mlstm_scan",
    )(p, p, p, p, g_rows, conv_w, conv_w, conv_b, conv_b, norm_g)


GLA_LEVELS = int(math.log2(LC))


def _gla_level_masks():
    j = np.arange(LC)[:, None]
    s = np.arange(LC)[None, :]
    out = []
    for fwd in (True, False):
        for lev in range(GLA_LEVELS):
            m = 1 << lev
            same = (j // (2 * m)) == (s // (2 * m))
            hi_j, hi_s = (j % (2 * m)) >= m, (s % (2 * m)) >= m
            out.append(same & (hi_j & ~hi_s if fwd else ~hi_j & hi_s))
    return np.stack(out).astype(np.float32)


def _seg_ref(b, m, fwd):
    n, w = b.shape
    r = m - 1 if fwd else m
    if 2 * m >= 8:
        b3 = b.reshape(n // (2 * m), 2 * m, w)
        return jnp.broadcast_to(b3[:, r:r + 1, :], b3.shape).reshape(n, w)
    b3 = b.reshape(n // 8, 8, w)
    sub = lax.broadcasted_iota(I32, b3.shape, 1)
    out = None
    for blk in range(8 // (2 * m)):
        row = jnp.broadcast_to(b3[:, blk * 2 * m + r:blk * 2 * m + r + 1, :], b3.shape)
        out = row if out is None else jnp.where(sub >= blk * 2 * m, row, out)
    return out.reshape(n, w)


def _gla_kernel(q_ref, k_ref, v_ref, g_ref, low_ref, cwq_ref, cwk_ref, cwv_ref, cbq_ref, cbk_ref, cbv_ref,
                gw_ref, gb_ref, ng_ref, mask_ref, y_ref, qs, ks, vs, yf, yb, st_scr, *, n_chunks):
    n_rows = n_chunks * LC
    qs[...] = (_conv_silu(q_ref, cwq_ref, cbq_ref, n_rows) * (B_DK ** -0.5)).astype(BF16)
    ks[...] = _conv_silu(k_ref, cwk_ref, cbk_ref, n_rows).astype(BF16)
    vs[...] = _conv_silu(v_ref, cwv_ref, cbv_ref, n_rows).astype(BF16)
    st_scr[...] = jnp.zeros_like(st_scr)
    tril, triu = _tri_masks()
    tril_b, triu_b = tril.astype(BF16), triu.astype(BF16)
    eye = jnp.logical_and(tril, triu).astype(F32)

    def step(cf, cb):
        dirs = (0, 1)
        fwd = (True, False)
        rows = [pl.ds(pl.multiple_of(c * LC, LC), LC) for c in (cf, cb)]
        st = [st_scr[d] for d in dirs]
        pre = [jnp.dot(low_ref[rows[d], :], gw_ref[d], precision=lax.Precision.HIGHEST,
                       preferred_element_type=F32) + gb_ref[d:d + 1, :] for d in dirs]
        lg = [_log_sigmoid(pre[d]) * (1.0 / B_TAU) for d in dirs]
        b = [_dot_exact_lhs01(tril_b if fwd[d] else triu_b, lg[d]) for d in dirs]
        q = [qs[rows[d], :].astype(F32) for d in dirs]
        k = [ks[rows[d], :].astype(F32) for d in dirs]
        v = [vs[rows[d], :] for d in dirs]
        a = [eye * jnp.sum(q[d] * k[d], axis=-1, keepdims=True) for d in dirs]
        for lev in range(GLA_LEVELS):
            for d in dirs:
                decay = jnp.exp(-jnp.abs(b[d] - _seg_ref(b[d], 1 << lev, fwd[d])))
                qt = (q[d] * decay).astype(BF16)
                kt = (k[d] * decay).astype(BF16)
                a[d] = a[d] + mask_ref[d * GLA_LEVELS + lev] * _dot_nt(qt, kt)
        intra = [_dot(a[d].astype(BF16), v[d]) for d in dirs]
        inter = [_dot_nt((q[d] * jnp.exp(b[d])).astype(BF16), st[d].astype(BF16)) for d in dirs]
        yf[rows[0], :] = intra[0] + inter[0]
        yb[rows[1], :] = intra[1] + inter[1]
        b_end = [b[0][LC - 1:LC, :], b[1][0:1, :]]
        upd = [_dot_tn(v[d], (k[d] * jnp.exp(b_end[d] - b[d])).astype(BF16)) for d in dirs]
        for d in dirs:
            st_scr[d] = st[d] * jnp.exp(b_end[d]) + upd[d]
    _scan_schedule(n_chunks, step)

    y = yf[...] + yb[...]
    yn = y * lax.rsqrt(jnp.mean(y * y, axis=-1, keepdims=True) + EPS)
    y_ref[...] = (yn * ng_ref[...] * _silu(g_ref[...].astype(F32))).astype(y_ref.dtype)


def gla_scan(p, low, conv_w, conv_b, gate_w, gate_b, norm_g, n_batch, n_chunks):
    rows = n_chunks * LC
    kq, kv, kg = B_HEADS, (2 * B_HEADS * B_DK) // B_DV, (2 * B_HEADS * B_DK + B_HEADS * B_DV) // B_DV
    masks = jnp.asarray(_gla_level_masks(), F32)
    cmap = lambda b, h: (0, 0, 0)
    return pl.pallas_call(
        functools.partial(_gla_kernel, n_chunks=n_chunks),
        grid=(n_batch, B_HEADS),
        in_specs=[pl.BlockSpec((rows, B_DK), lambda b, h: (b, h)),
                  pl.BlockSpec((rows, B_DK), lambda b, h: (b, kq + h)),
                  pl.BlockSpec((rows, B_DV), lambda b, h: (b, kv + h)),
                  pl.BlockSpec((rows, B_DV), lambda b, h: (b, kg + h)),
                  pl.BlockSpec((rows, 128), lambda b, h: (b, 0)),
                  pl.BlockSpec((3, B_DK), lambda b, h: (0, h)),
                  pl.BlockSpec((3, B_DK), lambda b, h: (0, kq + h)),
                  pl.BlockSpec((3, B_DV), lambda b, h: (0, kv + h)),
                  pl.BlockSpec((1, B_DK), lambda b, h: (0, h)),
                  pl.BlockSpec((1, B_DK), lambda b, h: (0, kq + h)),
                  pl.BlockSpec((1, B_DV), lambda b, h: (0, kv + h)),
                  pl.BlockSpec((2, 128, B_DK), lambda b, h: (0, 0, h)),
                  pl.BlockSpec((2, B_DK), lambda b, h: (0, h)),
                  pl.BlockSpec((1, B_DV), lambda b, h: (0, h)),
                  pl.BlockSpec((2 * GLA_LEVELS, LC, LC), cmap)],
        out_specs=pl.BlockSpec((rows, B_DV), lambda b, h: (b, h)),
        out_shape=jax.ShapeDtypeStruct((n_batch * rows, B_HEADS * B_DV), BF16),
        scratch_shapes=[pltpu.VMEM((rows, B_DK), BF16), pltpu.VMEM((rows, B_DK), BF16),
                        pltpu.VMEM((rows, B_DV), BF16),
                        pltpu.VMEM((rows, B_DV), F32), pltpu.VMEM((rows, B_DV), F32),
                        pltpu.VMEM((2, B_DV, B_DK), F32)],
        compiler_params=_cparams("parallel", "parallel"),
        name="gla_scan",
    )(p, p, p, p, low, conv_w, conv_w, conv_w, conv_b, conv_b, conv_b, gate_w, gate_b, norm_g, masks)


FB = 256


def _dft_mats(n_len):
    n = 2 * n_len
    nfb = n_len // FB
    f = np.arange(n_len)[:, None]
    t = np.arange(n_len)[None, :]
    ang = 2.0 * np.pi * ((f * t) % n) / n
    alt = (-1.0) ** np.arange(n_len)
    cf, sf = np.cos(ang), -np.sin(ang)
    sf[0, :] = alt
    fwd = np.stack([cf.reshape(nfb, FB, n_len), sf.reshape(nfb, FB, n_len)], axis=1).reshape(n, n_len)
    ci, si = (2.0 / n) * np.cos(ang.T), -(2.0 / n) * np.sin(ang.T)
    ci[:, 0] = 1.0 / n
    si[:, 0] = alt / n
    inv = np.stack([ci.reshape(n_len, nfb, FB), si.reshape(n_len, nfb, FB)], axis=2).reshape(n_len, n)
    return fwd.astype(np.float32), inv.astype(np.float32)


def _hy_pre_kernel(x0_ref, x1_ref, v_ref, cw0, cw1, cw2, cb0, cb1, cb2, zc_ref, zl_ref, x0c_ref, x0l_ref,
                   *, n_rows):
    x0 = _conv3(x0_ref, cw0, cb0, n_rows)
    z = _conv3(x1_ref, cw1, cb1, n_rows) * _conv3(v_ref, cw2, cb2, n_rows)
    zc_ref[...] = z[:LC].astype(BF16)
    zl_ref[...] = z[LC:].astype(BF16)
    x0c_ref[...] = x0[:LC].astype(BF16)
    x0l_ref[...] = x0[LC:].astype(BF16)


def hyena_pre(p, conv_w, conv_b, n_batch, n_chunks):
    rows = n_chunks * LC
    d = p.shape[1] // 3
    w = 256
    nj = d // w
    s_len = rows - LC
    spec = lambda k: pl.BlockSpec((rows, w), lambda b, j: (b, k * nj + j))
    cws = lambda k: pl.BlockSpec((3, w), lambda b, j: (0, k * nj + j))
    cbs = lambda k: pl.BlockSpec((1, w), lambda b, j: (0, k * nj + j))
    oc = pl.BlockSpec((LC, w), lambda b, j: (b, j))
    ol = pl.BlockSpec((s_len, w), lambda b, j: (b, j))
    return pl.pallas_call(
        functools.partial(_hy_pre_kernel, n_rows=rows),
        grid=(n_batch, nj),
        in_specs=[spec(0), spec(1), spec(2), cws(0), cws(1), cws(2), cbs(0), cbs(1), cbs(2)],
        out_specs=[oc, ol, oc, ol],
        out_shape=[jax.ShapeDtypeStruct((n_batch * LC, d), BF16), jax.ShapeDtypeStruct((n_batch * s_len, d), BF16),
                   jax.ShapeDtypeStruct((n_batch * LC, d), BF16), jax.ShapeDtypeStruct((n_batch * s_len, d), BF16)],
        compiler_params=_cparams("parallel", "parallel"),
        name="hyena_pre",
    )(p, p, p, conv_w, conv_w, conv_w, conv_b, conv_b, conv_b)


def _filt_dft_kernel(f_ref, x_ref, o_ref):
    acc = _dot(f_ref[...], x_ref[0])
    half = acc.shape[1] // 2
    o_ref[0] = acc[:, :half] + acc[:, half:]


def filter_spectrum(fwd, hsum, hdiff):
    n_len, d = hsum.shape

    def hilo(a):
        hi = a.astype(BF16)
        return jnp.concatenate([hi, (a - hi.astype(F32)).astype(BF16)], axis=1)
    xs = jnp.stack([hilo(hsum), hilo(hdiff)])
    return pl.pallas_call(
        _filt_dft_kernel,
        grid=(2, n_len // FB),
        in_specs=[pl.BlockSpec((FB, n_len), lambda w, m: (2 * m + w, 0)),
                  pl.BlockSpec((1, n_len, 2 * d), lambda w, m: (w, 0, 0))],
        out_specs=pl.BlockSpec((1, FB, d), lambda w, m: (w, m, 0)),
        out_shape=jax.ShapeDtypeStruct((2, n_len, d), F32),
        compiler_params=_cparams("parallel", "parallel"),
        name="filter_spectrum",
    )(fwd, xs)


def _dft_fwd_kernel(f_ref, z_ref, kr_ref, ki_ref, y_ref):
    acc = _dot(f_ref[...], z_ref[...])
    zr, zi = acc[:FB], acc[FB:]
    kr, ki = kr_ref[...], ki_ref[...]
    packed = jnp.logical_and(pl.program_id(1) == 0, lax.broadcasted_iota(I32, (FB, 1), 0) == 0)
    y_ref[:FB] = (zr * kr - jnp.where(packed, 0.0, zi * ki)).astype(BF16)
    y_ref[FB:] = jnp.where(packed, zi * ki, zr * ki + zi * kr).astype(BF16)


def dft_forward(fwd, z, kr, ki, n_batch):
    n_len, d = kr.shape
    nfb = n_len // FB
    return pl.pallas_call(
        _dft_fwd_kernel,
        grid=(n_batch, nfb),
        in_specs=[pl.BlockSpec((2 * FB, n_len), lambda b, m: (m, 0)),
                  pl.BlockSpec((n_len, d), lambda b, m: (b, 0)),
                  pl.BlockSpec((FB, d), lambda b, m: (m, 0)),
                  pl.BlockSpec((FB, d), lambda b, m: (m, 0))],
        out_specs=pl.BlockSpec((2 * FB, d), lambda b, m: (b * nfb + m, 0)),
        out_shape=jax.ShapeDtypeStruct((n_batch * 2 * n_len, d), BF16),
        compiler_params=_cparams("parallel", "arbitrary"),
        name="dft_forward",
    )(fwd, z, kr, ki)


def _dft_inv_kernel(ic_ref, yc_ref, zc_ref, x0c_ref, il_ref, yl_ref, zl_ref, x0l_ref, skip_ref, o_ref):
    def emit(i_ref, y_ref, z_ref, x0_ref):
        conv = _dot(i_ref[...], y_ref[...])
        o_ref[...] = (x0_ref[...].astype(F32) * (conv + z_ref[...].astype(F32) * skip_ref[...])).astype(o_ref.dtype)

    @pl.when(pl.program_id(1) == 0)
    def _():
        emit(ic_ref, yc_ref, zc_ref, x0c_ref)

    @pl.when(pl.program_id(1) > 0)
    def _():
        emit(il_ref, yl_ref, zl_ref, x0l_ref)


def dft_inverse(ctx_parts, lat_parts, skip, n_batch, tiles_per_batch):
    inv_c, yf_c, z_c, x0_c = ctx_parts
    inv_l, yf_l, z_l, x0_l = lat_parts
    d = z_l.shape[1]
    n_l = z_l.shape[0] // n_batch
    nt = n_l // TM
    lt = lambda m: jnp.maximum(m - 1, 0)
    return pl.pallas_call(
        _dft_inv_kernel,
        grid=(n_batch, tiles_per_batch),
        in_specs=[pl.BlockSpec((TM, 2 * LC), lambda b, m: (0, 0)),
                  pl.BlockSpec((2 * LC, d), lambda b, m: (b, 0)),
                  pl.BlockSpec((TM, d), lambda b, m: (b, 0)),
                  pl.BlockSpec((TM, d), lambda b, m: (b, 0)),
                  pl.BlockSpec((TM, 2 * n_l), lambda b, m: (lt(m), 0)),
                  pl.BlockSpec((2 * n_l, d), lambda b, m: (b, 0)),
                  pl.BlockSpec((TM, d), lambda b, m: (b * nt + lt(m), 0)),
                  pl.BlockSpec((TM, d), lambda b, m: (b * nt + lt(m), 0)),
                  pl.BlockSpec((1, d), lambda b, m: (0, 0))],
        out_specs=pl.BlockSpec((TM, d), lambda b, m: (b * tiles_per_batch + m, 0)),
        out_shape=jax.ShapeDtypeStruct((n_batch * tiles_per_batch * TM, d), BF16),
        compiler_params=_cparams("parallel", "arbitrary"),
        name="dft_inverse",
    )(inv_c, yf_c, z_c, x0_c, inv_l, yf_l, z_l, x0_l, skip.reshape(1, d))


def _hyena_filters(n_len, w1, b1, w_mid, b_mid, w_out, freq):
    hp = lax.Precision.HIGHEST
    d = w_out.shape[1] // 2
    n_bands = (w1.shape[0] - 1) // 2
    t = jnp.linspace(0.0, 1.0, n_len, dtype=F32)[:, None]
    pos = jnp.arange(n_len, dtype=F32)[:, None]
    bands = jnp.linspace(1e-4, n_bands - 1, n_bands, dtype=F32)[None, :]
    ang = (2.0 * math.pi / n_len) * pos * bands
    feats = jnp.concatenate([t, jnp.cos(ang), -jnp.sin(ang)], -1)
    h = jnp.sin(freq * (jnp.dot(feats, w1, precision=hp) + b1))
    for m in range(w_mid.shape[0]):
        h = jnp.sin(freq * (jnp.dot(h, w_mid[m], precision=hp) + b_mid[m]))
    h = jnp.dot(h, w_out, precision=hp).reshape(n_len, 2, d)
    deltas = jnp.abs(jnp.linspace(math.log(1e-2) / 1.5, math.log(1e-2) / 0.3, d, dtype=F32))
    h = h * jnp.exp(-t * deltas)[:, None, :]
    hf, hb = h[:, 0], h[:, 1]
    l1 = jnp.sum(jnp.abs(hf), 0) + jnp.sum(jnp.abs(hb[1:]), 0)
    return hf / l1, hb / l1


def hyena_mix(p, conv_w, conv_b, filt, skip, n_batch, n_chunks):
    rows = n_chunks * LC
    zc, zl, x0c, x0l = hyena_pre(p, conv_w, conv_b, n_batch, n_chunks)
    parts = []
    for z, x0 in ((zc, x0c), (zl, x0l)):
        n_len = z.shape[0] // n_batch
        fwd_np, inv_np = _dft_mats(n_len)
        fwd, inv = jnp.asarray(fwd_np, F32).astype(BF16), jnp.asarray(inv_np, F32).astype(BF16)
        hf, hb = _hyena_filters(n_len, *filt)
        hb0 = hb.at[0].set(0.0)
        hsum, hdiff = hf + hb0, hf - hb0
        spec = filter_spectrum(fwd, hsum, hdiff)
        alt = jnp.asarray((-1.0) ** np.arange(n_len), F32)[:, None]
        kr, ki = spec[0], spec[1].at[0].set(jnp.sum(alt * hsum, axis=0))
        parts.append((inv, dft_forward(fwd, z, kr, ki, n_batch), z, x0))
    return dft_inverse(parts[0], parts[1], skip, n_batch, n_chunks)


P_DTYPE = BF16
N_MAIN = 3072
N_EXTRA = 128


def _pad_cols(w, n):
    return jnp.pad(w, ((0, 0), (0, n - w.shape[1])))


def _to_scan_order(a, n_batch, ctx_len):
    w = a.shape[1]
    a3 = a.reshape(n_batch, -1, w)
    lat = a3[:, ctx_len:]
    s_len = lat.shape[1]
    lat = lat.reshape(n_batch, s_len // GRID_W, GRID_W, w).transpose(0, 2, 1, 3).reshape(n_batch, s_len, w)
    return jnp.concatenate([a3[:, :ctx_len], lat], axis=1).reshape(-1, w)


def _from_scan_order(a, n_batch, ctx_len):
    w = a.shape[1]
    a3 = a.reshape(n_batch, -1, w)
    lat = a3[:, ctx_len:]
    s_len = lat.shape[1]
    lat = lat.reshape(n_batch, GRID_W, s_len // GRID_W, w).transpose(0, 2, 1, 3).reshape(n_batch, s_len, w)
    return jnp.concatenate([a3[:, :ctx_len], lat], axis=1).reshape(-1, w)


def kernel(x, c, ctx, c_ctx, ada_w, ada_b, norm1_g, norm2_g, final_g, ml_w_in, ml_b_gate, ml_conv_w, ml_conv_b, ml_norm_g, ml_w_out, gla_w_in, gla_conv_w, gla_conv_b, gla_gate_w2, gla_gate_b, gla_norm_g, gla_w_out, hy_w_in, hy_b_in, hy_conv_w, hy_conv_b, hy_filt_w1, hy_filt_b1, hy_filt_w_mid, hy_filt_b_mid, hy_filt_w_out, hy_filt_freq, hy_skip, hy_w_out, hy_b_out, router_w, router_b, exp_w1, exp_w3, exp_w2):
    n_batch, s_len, d = x.shape
    ctx_len = ctx.shape[1]
    depth = ada_w.shape[0]
    assert ctx_len == LC == TM and s_len % LC == 0 and s_len % GRID_W == 0
    n_chunks = (ctx_len + s_len) // LC
    t_all = n_batch * n_chunks * LC
    lat_tiles = s_len // TM

    mod_rows = jnp.zeros((16, d), F32).at[0].set(c_ctx).at[1:1 + n_batch].set(c)
    mods = ada_mods(mod_rows, ada_w, ada_b).reshape(depth, 16, 6, d)
    xs = jnp.concatenate([ctx, x], axis=1).reshape(t_all, d)
    router_wt = router_w.T
    zeros_d = jnp.zeros((d,), F32)
    out = None

    for i in range(depth):
        last = i == depth - 1
        kind, j = i % 3, i // 3
        if kind == 0:
            w_in = _pad_cols(ml_w_in[j], N_MAIN + N_EXTRA).astype(BF16)
            bias = jnp.zeros((N_MAIN + N_EXTRA,), F32).at[N_MAIN:N_MAIN + 4 * A_HEADS].set(ml_b_gate[j])
            p, pg = in_proj(xs, mods[i], norm1_g[i], w_in, bias, N_MAIN, n_chunks, P_DTYPE)
            g4 = pg[:, :4 * A_HEADS].reshape(t_all, 4, A_HEADS // 2, 2)
            g_cols = g4.transpose(2, 0, 1, 3).reshape(A_HEADS // 2, t_all, 8)
            g_rows = g_cols.reshape(A_HEADS // 2, n_batch, n_chunks, LC, 8).transpose(0, 1, 2, 4, 3)
            y = mlstm_scan(p, g_rows, ml_conv_w[j], ml_conv_b[j].reshape(1, -1),
                           ml_norm_g[j].reshape(1, -1), n_batch, n_chunks)
            w_out, b_out = ml_w_out[j], zeros_d
        elif kind == 1:
            w_in = _pad_cols(gla_w_in[j], N_MAIN + N_EXTRA).astype(BF16)
            bias = jnp.zeros((N_MAIN + N_EXTRA,), F32)
            p, pg = in_proj(_to_scan_order(xs, n_batch, ctx_len), mods[i], norm1_g[i], w_in, bias, N_MAIN,
                            n_chunks, P_DTYPE)
            gate_w = jnp.zeros((2, 128, B_HEADS * B_DK), F32)
            gate_w = gate_w.at[0, :B_RANK].set(gla_gate_w2[j, 0]).at[1, B_RANK:2 * B_RANK].set(gla_gate_w2[j, 1])
            y = gla_scan(p, pg, gla_conv_w[j], gla_conv_b[j].reshape(1, -1), gate_w, gla_gate_b[j],
                         gla_norm_g[j].reshape(1, -1), n_batch, n_chunks)
            y = _from_scan_order(y, n_batch, ctx_len)
            w_out, b_out = gla_w_out[j], zeros_d
        else:
            (p,) = in_proj(xs, mods[i], norm1_g[i], hy_w_in[j].astype(BF16), hy_b_in[j], N_MAIN, n_chunks, P_DTYPE)
            filt = (hy_filt_w1[j], hy_filt_b1[j], hy_filt_w_mid[j], hy_filt_b_mid[j], hy_filt_w_out[j],
                    hy_filt_freq[j])
            y = hyena_mix(p, hy_conv_w[j], hy_conv_b[j].reshape(1, -1), filt, hy_skip[j], n_batch, n_chunks)
            w_out, b_out = hy_w_out[j], hy_b_out[j]

        x2, h2, ids, gates, ranks, counts = out_route(
            xs, y, mods[i], w_out.astype(BF16), b_out, norm2_g[i], router_wt, router_b, n_chunks, 1, last)
        n_tok = x2.shape[0]
        dest_tiles, n_slots, block_exp, n_used, pad_end = moe_plan(ids, ranks, counts, n_tok)
        xb = moe_dispatch(h2, dest_tiles, pad_end, n_used, n_slots)
        yb = moe_experts(xb, block_exp, n_used, exp_w1, exp_w3, exp_w2, i)
        if last:
            out = moe_combine(x2, yb, dest_tiles, gates.T, mods[i], lambda t: 1 + t // lat_tiles, final_g)
        else:
            xs = moe_combine(x2, yb, dest_tiles, gates.T, mods[i],
                             lambda t: jnp.where(t % n_chunks == 0, 0, 1 + t // n_chunks))
    return out.reshape(n_batch, s_len, d)
```

```python
import functools
import math

import numpy as np
import jax
import jax.numpy as jnp
from jax import lax
from jax.experimental import pallas as pl
from jax.experimental.pallas import tpu as pltpu

F32 = jnp.float32
BF16 = jnp.bfloat16
I32 = jnp.int32

EPS = 1e-6
TM = 256
LC = 256
MOE_BLOCK = 512
N_EXPERTS = 16
N_GROUPS = 4
EPG = N_EXPERTS // N_GROUPS
A_HEADS, A_DK, A_DV = 8, 64, 128
B_HEADS, B_DK, B_DV = 4, 128, 256
B_RANK = 16
B_TAU = 16.0
GRID_W = 64
VMEM_LIMIT_V7X = 56 * 1024 * 1024


def _cparams(*sem):
    return pltpu.CompilerParams(dimension_semantics=sem, vmem_limit_bytes=VMEM_LIMIT_V7X)


def _silu(x):
    return x * (1.0 / (1.0 + jnp.exp(-x)))


def _sigmoid(x):
    return 1.0 / (1.0 + jnp.exp(-x))


def _log_sigmoid(x):
    return jnp.minimum(x, 0.0) - jnp.log(1.0 + jnp.exp(-jnp.abs(x)))


def _split3(x):
    hi = x.astype(BF16)
    r1 = x - hi.astype(F32)
    mid = r1.astype(BF16)
    lo = (r1 - mid.astype(F32)).astype(BF16)
    return hi, mid, lo


def _dot(a, b):
    return jnp.dot(a, b, preferred_element_type=F32)


def _dot_nt(a, b):
    return lax.dot_general(a, b, (((1,), (1,)), ((), ())), preferred_element_type=F32)


def _dot_tn(a, b):
    return lax.dot_general(a, b, (((0,), (0,)), ((), ())), preferred_element_type=F32)


LANES = 128
SUBLANES = 8


def _tok_tiles_store(ref, base, x):
    n, d = x.shape
    per = d // LANES
    for c in range(per):
        ref[pl.ds(base + c, n, stride=per), :] = x[:, c * LANES:(c + 1) * LANES]


def _tok_tiles_load(ref, base, n, d):
    per = d // LANES
    return jnp.concatenate([ref[pl.ds(base + c, n, stride=per), :] for c in range(per)], axis=1)


def _dot_exact_lhs01(m01, x):
    hi, mid, lo = _split3(x)
    return (_dot(m01, hi) + _dot(m01, mid)) + _dot(m01, lo)


def _dot_exact_rhs01(x, m01):
    hi, mid, lo = _split3(x)
    return (_dot(hi, m01) + _dot(mid, m01)) + _dot(lo, m01)


def _ada_kernel(s_ref, w_ref, b_ref, o_ref):
    s = _silu(s_ref[...]).astype(BF16)
    o_ref[0] = _dot(s, w_ref[0].astype(BF16)) + b_ref[0]


def ada_mods(rows, ada_w, ada_b):
    depth, d, n6 = ada_w.shape
    nr = rows.shape[0]
    tn = 1536
    return pl.pallas_call(
        _ada_kernel,
        grid=(depth, n6 // tn),
        in_specs=[pl.BlockSpec((nr, d), lambda l, j: (0, 0)),
                  pl.BlockSpec((1, d, tn), lambda l, j: (l, 0, j)),
                  pl.BlockSpec((1, 1, tn), lambda l, j: (l, 0, j))],
        out_specs=pl.BlockSpec((1, nr, tn), lambda l, j: (l, 0, j)),
        out_shape=jax.ShapeDtypeStruct((depth, nr, n6), F32),
        compiler_params=_cparams("parallel", "parallel"),
        name="ada_mods",
    )(rows, ada_w, ada_b.reshape(depth, 1, n6))


def _norm_mod(x, g, shift, scale):
    y = x * lax.rsqrt(jnp.mean(x * x, axis=-1, keepdims=True) + EPS) * g
    return y * (1.0 + scale) + shift


def _in_proj_kernel(x_ref, mod_ref, g_ref, w_ref, b_ref, p_ref, *rest, n_main):
    h = _norm_mod(x_ref[...], g_ref[...], mod_ref[0, 0:1, :], mod_ref[0, 1:2, :]).astype(BF16)
    p_ref[...] = (_dot(h, w_ref[:, :n_main]) + b_ref[:, :n_main]).astype(p_ref.dtype)
    if rest:
        rest[0][...] = _dot(h, w_ref[:, n_main:]) + b_ref[:, n_main:]


def in_proj(x, mods_l, g, w, b, n_main, tiles_per_batch, p_dtype):
    t, d = x.shape
    n = w.shape[1]
    n_extra = n - n_main
    mod_row = lambda i: jnp.where(i % tiles_per_batch == 0, 0, 1 + i // tiles_per_batch)
    out_shape = [jax.ShapeDtypeStruct((t, n_main), p_dtype)]
    out_specs = [pl.BlockSpec((TM, n_main), lambda i: (i, 0))]
    if n_extra:
        out_shape.append(jax.ShapeDtypeStruct((t, n_extra), F32))
        out_specs.append(pl.BlockSpec((TM, n_extra), lambda i: (i, 0)))
    return pl.pallas_call(
        functools.partial(_in_proj_kernel, n_main=n_main),
        grid=(t // TM,),
        in_specs=[pl.BlockSpec((TM, d), lambda i: (i, 0)),
                  pl.BlockSpec((1, 6, d), lambda i: (mod_row(i), 0, 0)),
                  pl.BlockSpec((1, d), lambda i: (0, 0)),
                  pl.BlockSpec((d, n), lambda i: (0, 0)),
                  pl.BlockSpec((1, n), lambda i: (0, 0))],
        out_specs=out_specs,
        out_shape=out_shape,
        compiler_params=_cparams("parallel"),
        name="in_proj",
    )(x, mods_l, g.reshape(1, d), w, b.reshape(1, n))


def _first_argmax(vals):
    best, idx = vals[0], jnp.zeros(vals[0].shape, I32)
    for i in range(1, len(vals)):
        take = vals[i] > best
        best = jnp.where(take, vals[i], best)
        idx = jnp.where(take, i, idx)
    return best, idx


def _select_row(rows, idx):
    out = rows[0]
    for i in range(1, len(rows)):
        out = jnp.where(idx == i, rows[i], out)
    return out


def _route(scores, sel):
    neg = jnp.float32(-jnp.inf)
    srow = [sel[e:e + 1, :] for e in range(N_EXPERTS)]
    crow = [scores[e:e + 1, :] for e in range(N_EXPERTS)]
    gscore = []
    for g in range(N_GROUPS):
        a = srow[g * EPG:(g + 1) * EPG]
        m1, i1 = _first_argmax(a)
        m2, _ = _first_argmax([jnp.where(i1 == i, neg, a[i]) for i in range(EPG)])
        gscore.append(m1 + m2)
    _, grp = _first_argmax(gscore)
    in_sel = [_select_row([srow[g * EPG + i] for g in range(N_GROUPS)], grp) for i in range(EPG)]
    in_sc = [_select_row([crow[g * EPG + i] for g in range(N_GROUPS)], grp) for i in range(EPG)]
    _, l1 = _first_argmax(in_sel)
    _, l2 = _first_argmax([jnp.where(l1 == i, neg, in_sel[i]) for i in range(EPG)])
    g1 = _select_row(in_sc, l1)
    g2 = _select_row(in_sc, l2)
    tot = g1 + g2
    ids = jnp.concatenate([grp * EPG + l1, grp * EPG + l2], axis=0)
    gates = jnp.concatenate([g1 / tot, g2 / tot], axis=0)
    return ids, gates


ROUTE_SUB = 4


def _out_route_kernel(*refs, mod_row):
    x_refs, y_refs = refs[:ROUTE_SUB], refs[ROUTE_SUB:2 * ROUTE_SUB]
    (mod_ref, w_ref, b_ref, g_ref, rw_ref, rb_ref,
     xo_ref, h_ref, ids_ref, gates_ref, rank_ref, cnt_ref, carry_ref) = refs[2 * ROUTE_SUB:]
    i = pl.program_id(0)

    @pl.when(i == 0)
    def _():
        carry_ref[...] = jnp.zeros_like(carry_ref)

    halves = tuple(range(ROUTE_SUB))
    rows = [slice(k * TM, (k + 1) * TM) for k in halves]
    mod = [mod_ref[mod_row(ROUTE_SUB * i + k)] for k in halves]
    proj = [_dot(y_ref[...], w_ref[...]) for y_ref in y_refs]
    xn = [x_ref[...] + mod[k][2:3, :] * (proj[k] + b_ref[...]) for k, x_ref in enumerate(x_refs)]
    h = [_norm_mod(xn[k], g_ref[...], mod[k][3:4, :], mod[k][4:5, :]) for k in halves]
    rw = rw_ref[...]
    rw_hi = rw.astype(BF16)
    rw_2 = jnp.concatenate([rw_hi, (rw - rw_hi.astype(F32)).astype(BF16)], axis=0)
    h_hi = [h[k].astype(BF16) for k in halves]
    h_lo = [(h[k] - h_hi[k].astype(F32)).astype(BF16) for k in halves]
    l_1 = [_dot_nt(rw_2, h_hi[k]) for k in halves]
    l_2 = [_dot_nt(rw_hi, h_lo[k]) for k in halves]
    logits = [(l_1[k][:N_EXPERTS] + l_1[k][N_EXPERTS:]) + l_2[k] for k in halves]
    for k in halves:
        xo_ref[rows[k], :] = xn[k]
        _tok_tiles_store(h_ref, k * TM * SUBLANES, h[k])
    scores = [_sigmoid(logits[k]) for k in halves]
    routes = [_route(scores[k], scores[k] + rb_ref[:, 0:1]) for k in halves]
    erow = lax.broadcasted_iota(I32, (N_EXPERTS, TM), 0)
    before = (lax.broadcasted_iota(I32, (TM, TM), 0) < lax.broadcasted_iota(I32, (TM, TM), 1)).astype(BF16)
    oh = [[(erow == routes[k][0][j:j + 1, :]).astype(F32) for j in range(2)] for k in halves]
    pre = [[_dot(oh[k][j].astype(BF16), before) for j in range(2)] for k in halves]
    tot = [[jnp.sum(oh[k][j], axis=1, keepdims=True) for j in range(2)] for k in halves]

    carry = carry_ref[:, 0:1]
    for k in halves:
        ids_ref[:, rows[k]] = routes[k][0]
        gates_ref[:, rows[k]] = routes[k][1]
        r0 = jnp.sum(oh[k][0] * (carry + pre[k][0]), axis=0, keepdims=True)
        r1 = jnp.sum(oh[k][1] * (carry + tot[k][0] + pre[k][1]), axis=0, keepdims=True)
        rank_ref[:, rows[k]] = jnp.concatenate([r0, r1], axis=0).astype(I32)
        carry = carry + tot[k][0] + tot[k][1]
    carry_ref[...] = jnp.broadcast_to(carry, carry_ref.shape)
    cnt_ref[...] = jnp.broadcast_to(carry, cnt_ref.shape).astype(I32)


def out_route(x, y, mods_l, w_out, b_out, g2n, router_wt, router_b, tiles_per_batch, ctx_tiles, skip_ctx):
    t, d = x.shape
    nb = t // (tiles_per_batch * TM)
    if skip_ctx:
        per = tiles_per_batch - ctx_tiles
        src = lambda i: (i // per) * tiles_per_batch + ctx_tiles + i % per
        mod_row = lambda i: 1 + i // per
        n_tiles = nb * per
    else:
        src = lambda i: i
        mod_row = lambda i: jnp.where(i % tiles_per_batch < ctx_tiles, 0, 1 + i // tiles_per_batch)
        n_tiles = nb * tiles_per_batch
    ns = ROUTE_SUB
    assert n_tiles % ns == 0 and d == SUBLANES * LANES
    tr = n_tiles * TM
    tok = lambda i: (i, 0)
    row2 = lambda i: (0, i)
    tile_specs = [pl.BlockSpec((TM, d), functools.partial(lambda i, k: (src(ns * i + k), 0), k=k))
                  for k in range(ns)]
    outs = pl.pallas_call(
        functools.partial(_out_route_kernel, mod_row=mod_row),
        grid=(n_tiles // ns,),
        in_specs=tile_specs + tile_specs + [
            pl.BlockSpec(mods_l.shape, lambda i: (0, 0, 0)),
            pl.BlockSpec((d, d), lambda i: (0, 0)),
            pl.BlockSpec((1, d), lambda i: (0, 0)),
            pl.BlockSpec((1, d), lambda i: (0, 0)),
            pl.BlockSpec((N_EXPERTS, d), lambda i: (0, 0)),
            pl.BlockSpec((N_EXPERTS, 128), lambda i: (0, 0))],
        out_specs=[pl.BlockSpec((ns * TM, d), tok),
                   pl.BlockSpec((ns * TM * SUBLANES, LANES), tok),
                   pl.BlockSpec((2, ns * TM), row2),
                   pl.BlockSpec((2, ns * TM), row2),
                   pl.BlockSpec((2, ns * TM), row2),
                   pl.BlockSpec((N_EXPERTS, 128), lambda i: (0, 0))],
        out_shape=[jax.ShapeDtypeStruct((tr, d), F32),
                   jax.ShapeDtypeStruct((tr * SUBLANES, LANES), F32),
                   jax.ShapeDtypeStruct((2, tr), I32),
                   jax.ShapeDtypeStruct((2, tr), F32),
                   jax.ShapeDtypeStruct((2, tr), I32),
                   jax.ShapeDtypeStruct((N_EXPERTS, 128), I32)],
        scratch_shapes=[pltpu.VMEM((N_EXPERTS, 128), F32)],
        compiler_params=_cparams("arbitrary"),
        name="out_route",
    )(*([x] * ns + [y] * ns), mods_l, w_out, b_out.reshape(1, d), g2n.reshape(1, d), router_wt,
      jnp.broadcast_to(router_b.reshape(N_EXPERTS, 1), (N_EXPERTS, 128)))
    return outs


def _moe_kernel(bexp_ref, nused_ref, x_ref, w1_ref, w3_ref, w2_ref, o_ref, w1b, w3b, w2b):
    i = pl.program_id(0)
    prev = bexp_ref[jnp.maximum(i - 1, 0)]
    fresh = jnp.logical_or(i == 0, bexp_ref[i] != prev)
    used = i < nused_ref[0]

    @pl.when(jnp.logical_and(fresh, used))
    def _():
        w1b[...] = w1_ref[0].astype(BF16)
        w3b[...] = w3_ref[0].astype(BF16)
        w2b[...] = w2_ref[0].astype(BF16)

    @pl.when(used)
    def _():
        x = _tok_tiles_load(x_ref, 0, MOE_BLOCK, w1b.shape[0]).astype(BF16)
        a = _dot(x, w1b[...])
        b = _dot(x, w3b[...])
        hmid = (_silu(a) * b).astype(BF16)
        _tok_tiles_store(o_ref, 0, _dot(hmid, w2b[...]))

    @pl.when(jnp.logical_not(used))
    def _():
        o_ref[...] = jnp.zeros_like(o_ref)


def moe_experts(xb, block_exp, n_used, w1, w3, w2, layer):
    d, ff = w1.shape[2], w1.shape[3]
    per = d // LANES
    n_slots, dh = xb.shape[0] // per, LANES
    n_blocks = n_slots // MOE_BLOCK
    wmap = lambda i, be, nu: (layer, be[i], 0, 0)
    return pl.pallas_call(
        _moe_kernel,
        grid_spec=pltpu.PrefetchScalarGridSpec(
            num_scalar_prefetch=2,
            grid=(n_blocks,),
            in_specs=[pl.BlockSpec((MOE_BLOCK * per, dh), lambda i, be, nu: (i, 0)),
                      pl.BlockSpec((None, 1, d, ff), wmap),
                      pl.BlockSpec((None, 1, d, ff), wmap),
                      pl.BlockSpec((None, 1, ff, d), wmap)],
            out_specs=pl.BlockSpec((MOE_BLOCK * per, dh), lambda i, be, nu: (i, 0)),
            scratch_shapes=[pltpu.VMEM((d, ff), BF16), pltpu.VMEM((d, ff), BF16),
                            pltpu.VMEM((ff, d), BF16)]),
        out_shape=jax.ShapeDtypeStruct((n_slots * per, dh), F32),
        compiler_params=_cparams("arbitrary"),
        name="moe_experts",
    )(block_exp, n_used, xb, w1, w3, w2)


def moe_plan(ids, ranks, counts, n_tokens):
    n_assign = 2 * n_tokens
    n_blocks = n_assign // MOE_BLOCK + N_EXPERTS
    cnt = counts[:, 0]
    padded = (cnt + MOE_BLOCK - 1) // MOE_BLOCK * MOE_BLOCK
    pad_end = jnp.cumsum(padded)
    pad_start = pad_end - padded
    start_of = sum(jnp.where(ids == e, pad_start[e], 0) for e in range(N_EXPERTS))
    dest = start_of + ranks
    blk_start = jnp.arange(n_blocks, dtype=I32) * MOE_BLOCK
    block_exp = jnp.minimum(jnp.sum(blk_start[:, None] >= pad_end[None, :], axis=1), N_EXPERTS - 1).astype(I32)
    n_used = (pad_end[-1] // MOE_BLOCK).astype(I32).reshape(1)
    dest_tiles = dest.reshape(2, n_tokens // TM, TM).transpose(1, 0, 2)
    return dest_tiles, n_blocks * MOE_BLOCK, block_exp, n_used, pad_end.astype(I32)


def _tile_of(idx):
    return pl.ds(pl.multiple_of(idx * SUBLANES, SUBLANES), SUBLANES)


def _token_copies(src_of, dst_of, dest_ref, sem):
    def body(r, carry):
        for k in range(2):
            pltpu.make_async_copy(src_of(k, r, dest_ref[0, k, r]), dst_of(k, r, dest_ref[0, k, r]), sem).start()
        return carry
    lax.fori_loop(0, TM, body, 0, unroll=8)


def _dispatch_kernel(pad_end_ref, nused_ref, dest_ref, h_ref, xb_ref, zeros, ring, sem, zsem, *, n_blocks):
    i = pl.program_id(0)
    n = pl.num_programs(0)
    blk_rows = MOE_BLOCK * SUBLANES

    def zero_block(blk):
        return pltpu.make_async_copy(zeros, xb_ref.at[pl.ds(pl.multiple_of(blk * blk_rows, blk_rows), blk_rows)],
                                     zsem)

    @pl.when(i == 0)
    def _():
        zeros[...] = jnp.zeros_like(zeros)
        fills = []
        for e in range(N_EXPERTS):
            end = pad_end_ref[e]
            start = pad_end_ref[e - 1] if e else 0
            fills.append((end > start, end // MOE_BLOCK - 1))
        for j in range(N_EXPERTS):
            fills.append((nused_ref[0] + j < n_blocks, nused_ref[0] + j))
        for cond, blk in fills:
            @pl.when(cond)
            def _():
                zero_block(blk).start()
        for cond, blk in fills:
            @pl.when(cond)
            def _():
                zero_block(blk).wait()

    def tile_wait(slot):
        for _ in range(2):
            pltpu.make_async_copy(h_ref, xb_ref.at[pl.ds(0, TM * SUBLANES)], sem.at[slot]).wait()

    slot = i % 2

    @pl.when(i >= 2)
    def _():
        tile_wait(slot)
    ring[slot] = h_ref[...]
    _token_copies(lambda k, r, s: ring.at[slot, _tile_of(r)], lambda k, r, s: xb_ref.at[_tile_of(s)], dest_ref,
                  sem.at[slot])

    @pl.when(i == n - 1)
    def _():
        @pl.when(n >= 2)
        def _():
            tile_wait(1 - slot)
        tile_wait(slot)


def moe_dispatch(h, dest_tiles, pad_end, n_used, n_slots):
    rows, dh = h.shape
    return pl.pallas_call(
        functools.partial(_dispatch_kernel, n_blocks=n_slots // MOE_BLOCK),
        grid_spec=pltpu.PrefetchScalarGridSpec(
            num_scalar_prefetch=2,
            grid=(rows // (TM * SUBLANES),),
            in_specs=[pl.BlockSpec((1, 2, TM), lambda i, pe, nu: (i, 0, 0), memory_space=pltpu.SMEM),
                      pl.BlockSpec((TM * SUBLANES, dh), lambda i, pe, nu: (i, 0))],
            out_specs=pl.BlockSpec(memory_space=pl.ANY),
            scratch_shapes=[pltpu.VMEM((MOE_BLOCK * SUBLANES, dh), h.dtype),
                            pltpu.VMEM((2, TM * SUBLANES, dh), h.dtype),
                            pltpu.SemaphoreType.DMA((2,)), pltpu.SemaphoreType.DMA(())]),
        out_shape=jax.ShapeDtypeStruct((n_slots * SUBLANES, dh), h.dtype),
        compiler_params=_cparams("arbitrary"),
        name="moe_dispatch",
    )(pad_end, n_used, dest_tiles, h)


def _combine_kernel(dcur_ref, dnext_ref, x_ref, yb_ref, gt_ref, mod_ref, *rest, final):
    buf, sem = rest[-2:]
    i = pl.program_id(0)
    n = pl.num_programs(0)

    def fetch(dest_ref, slot):
        _token_copies(lambda k, r, s: yb_ref.at[_tile_of(s)], lambda k, r, s: buf.at[slot, k, _tile_of(r)],
                      dest_ref, sem.at[slot])

    @pl.when(i == 0)
    def _():
        fetch(dcur_ref, 0)

    @pl.when(i + 1 < n)
    def _():
        fetch(dnext_ref, (i + 1) % 2)

    slot = i % 2
    for k in range(2):
        pltpu.make_async_copy(yb_ref.at[pl.ds(0, TM * SUBLANES)], buf.at[slot, k], sem.at[slot]).wait()
    d = x_ref.shape[1]
    moe = (gt_ref[:, 0:1] * _tok_tiles_load(buf.at[slot, 0], 0, TM, d)
           + gt_ref[:, 1:2] * _tok_tiles_load(buf.at[slot, 1], 0, TM, d))
    xn = x_ref[...] + mod_ref[0, 5:6, :] * moe
    if final:
        g_ref, o_ref = rest[:2]
        o_ref[...] = xn * lax.rsqrt(jnp.mean(xn * xn, axis=-1, keepdims=True) + EPS) * g_ref[...]
    else:
        rest[0][...] = xn


def moe_combine(x, yb, dest_tiles, gates_t, mods_l, mod_row, final_g=None):
    t, d = x.shape
    n_tiles = t // TM
    tok = lambda i: (i, 0)
    in_specs = [pl.BlockSpec((1, 2, TM), lambda i: (i, 0, 0), memory_space=pltpu.SMEM),
                pl.BlockSpec((1, 2, TM), lambda i: (jnp.minimum(i + 1, n_tiles - 1), 0, 0), memory_space=pltpu.SMEM),
                pl.BlockSpec((TM, d), tok),
                pl.BlockSpec(memory_space=pl.ANY),
                pl.BlockSpec((TM, 2), tok),
                pl.BlockSpec((1, 6, d), lambda i: (mod_row(i), 0, 0))]
    args = [dest_tiles, dest_tiles, x, yb, gates_t, mods_l]
    if final_g is not None:
        in_specs.append(pl.BlockSpec((1, d), lambda i: (0, 0)))
        args.append(final_g.reshape(1, d))
    return pl.pallas_call(
        functools.partial(_combine_kernel, final=final_g is not None),
        grid=(n_tiles,),
        in_specs=in_specs,
        out_specs=pl.BlockSpec((TM, d), tok),
        out_shape=jax.ShapeDtypeStruct((t, d), F32),
        scratch_shapes=[pltpu.VMEM((2, 2, TM * SUBLANES, LANES), F32), pltpu.SemaphoreType.DMA((2,))],
        compiler_params=_cparams("arbitrary"),
        name="moe_combine",
    )(*args)


def _conv3(x_ref, cw_ref, cb_ref, n_rows):
    x = x_ref[...].astype(F32)
    row = lax.broadcasted_iota(I32, (n_rows, 1), 0)
    first = jnp.logical_or(row == 0, row == LC)
    last = jnp.logical_or(row == LC - 1, row == n_rows - 1)
    prev = jnp.where(first, 0.0, pltpu.roll(x, 1, 0))
    nxt = jnp.where(last, 0.0, pltpu.roll(x, n_rows - 1, 0))
    return prev * cw_ref[0:1, :] + x * cw_ref[1:2, :] + nxt * cw_ref[2:3, :] + cb_ref[...]


def _conv_silu(x_ref, cw_ref, cb_ref, n_rows):
    return _silu(_conv3(x_ref, cw_ref, cb_ref, n_rows))


def _tri_masks():
    ri = lax.broadcasted_iota(I32, (LC, LC), 0)
    ci = lax.broadcasted_iota(I32, (LC, LC), 1)
    return ri >= ci, ri <= ci


def _scan_schedule(n_chunks, step):
    step(0, 0)

    def body(t, carry):
        step(t, n_chunks - t)
        return carry
    lax.fori_loop(1, n_chunks, body, 0)


def _rows16(rows):
    n = rows[0].shape[1]
    ri = lax.broadcasted_iota(I32, (16, n), 0)
    out = jnp.zeros((16, n), F32)
    for i, r in enumerate(rows):
        out = jnp.where(ri == i, jnp.broadcast_to(r, (16, n)), out)
    return out.astype(BF16)


def _mlstm_kernel(q_ref, k_ref, v_ref, o_ref, gt_ref, cwq_ref, cwk_ref, cbq_ref, cbk_ref, ng_ref,
                  y_ref, qs, ks, kst, vt, yft, ybt, ct_scr, m_scr, *, n_chunks):
    n_rows = n_chunks * LC
    qs[...] = (_conv_silu(q_ref, cwq_ref, cbq_ref, n_rows) * (A_DK ** -0.5)).astype(BF16)
    kf = _conv_silu(k_ref, cwk_ref, cbk_ref, n_rows)
    ks[...] = kf.astype(BF16)
    for c in range(n_chunks):
        kst[c] = kf[c * LC:(c + 1) * LC, :].T.astype(BF16)
        vt[c] = v_ref[c * LC:(c + 1) * LC, :].astype(F32).T.astype(BF16)
    ct_scr[...] = jnp.zeros_like(ct_scr)
    m_scr[...] = jnp.zeros_like(m_scr)
    tril, triu = _tri_masks()
    tril_b, triu_b = tril.astype(BF16), triu.astype(BF16)
    ones_rows = (lax.broadcasted_iota(I32, (A_DV, LC), 0) == 0).astype(BF16)
    one = jnp.ones((1, LC), F32)
    neg = jnp.float32(-jnp.inf)

    f32 = lambda t: t.astype(F32)

    def step(cf, cb):
        chains = [(cf, True, 0), (cf, True, 1), (cb, False, 0), (cb, False, 1)]
        m_prev = [m_scr[i, 0:1, 0:1] for i in range(4)]
        ct = [ct_scr[i] for i in range(4)]
        gts = {True: gt_ref[0, 0, cf], False: gt_ref[0, 0, cb]}
        css = {True: _dot_exact_rhs01(_log_sigmoid(gts[True]), triu_b),
               False: _dot_exact_rhs01(_log_sigmoid(gts[False]), tril_b)}
        rows = [pl.ds(pl.multiple_of(c * LC, LC), LC) for c, _, _ in chains]
        qh = [qs[rows[i], hh * A_DK:(hh + 1) * A_DK] for i, (_, _, hh) in enumerate(chains)]
        kh = [ks[rows[i], hh * A_DK:(hh + 1) * A_DK] for i, (_, _, hh) in enumerate(chains)]
        qk_t = [_dot_nt(kh[i], qh[i]) for i in range(4)]
        fcum, g, log_d = [], [], []
        for c, fwd, hh in chains:
            d = 0 if fwd else 1
            fc_row = css[fwd][4 * d + 2 + hh:4 * d + 3 + hh, :]
            g_row = gts[fwd][4 * d + hh:4 * d + hh + 1, :] - fc_row
            f_hi, f_mid, f_lo = _split3(fc_row)
            g_hi, g_mid, g_lo = _split3(g_row)
            f_slab = _rows16([f32(f_hi), f32(f_mid), f32(f_lo), one, one, one])
            g_slab = _rows16([one, one, one, f32(g_hi), f32(g_mid), f32(g_lo)])
            log_d.append(jnp.where(triu if fwd else tril, _dot_tn(g_slab, f_slab), neg))
            fcum.append(fc_row)
            g.append(g_row)
        li = [fcum[i] + m_prev[i] for i in range(4)]
        m_row = [jnp.maximum(li[i], jnp.max(log_d[i], axis=0, keepdims=True)) for i in range(4)]
        sm_t = [(qk_t[i] * jnp.exp(log_d[i] - m_row[i])).astype(BF16) for i in range(4)]
        vext_t = [jnp.concatenate([vt[c, hh * A_DV:(hh + 1) * A_DV, :], ones_rows], axis=0)
                  for c, _, hh in chains]
        intra = [_dot(vext_t[i], sm_t[i]) for i in range(4)]
        inter = [_dot_nt(ct[i].astype(BF16), qh[i]) for i in range(4)]
        kw_t, m_new = [], []
        for i, (c, fwd, hh) in enumerate(chains):
            num_t = intra[i] + jnp.exp(li[i] - m_row[i]) * inter[i]
            den = num_t[A_DV:A_DV + 1, :]
            (yft if fwd else ybt)[c, hh * A_DV:(hh + 1) * A_DV, :] = num_t[:A_DV, :] * (
                1.0 / jnp.maximum(jnp.abs(den), jnp.exp(-m_row[i])))
            b_end = fcum[i][:, LC - 1:LC] if fwd else fcum[i][:, 0:1]
            m_new.append(jnp.maximum(b_end + m_prev[i], jnp.max(b_end + g[i], axis=1, keepdims=True)))
            kw_t.append((kst[c, hh * A_DK:(hh + 1) * A_DK, :].astype(F32)
                         * jnp.exp(b_end + g[i] - m_new[i])).astype(BF16))
            ct[i] = jnp.exp(b_end + m_prev[i] - m_new[i]) * ct[i]
        upd = [_dot_nt(vext_t[i], kw_t[i]) for i in range(4)]
        for i in range(4):
            ct_scr[i] = ct[i] + upd[i]
            m_scr[i] = jnp.broadcast_to(m_new[i], m_scr.shape[1:])
    _scan_schedule(n_chunks, step)

    for hh in range(2):
        sl = slice(hh * A_DV, (hh + 1) * A_DV)
        for c in range(n_chunks):
            y_t = yft[c, sl, :] + ybt[c, sl, :]
            yn = (y_t * lax.rsqrt(jnp.mean(y_t * y_t, axis=0, keepdims=True) + EPS)).T
            rs = slice(c * LC, (c + 1) * LC)
            y_ref[rs, sl] = (yn * ng_ref[:, sl] * _sigmoid(o_ref[rs, sl].astype(F32))).astype(y_ref.dtype)


def mlstm_scan(p, g_rows, conv_w, conv_b, norm_g, n_batch, n_chunks):
    rows = n_chunks * LC
    n_hp = A_HEADS // 2
    qk_w = 2 * A_DK
    v_w = 2 * A_DV
    kq, kv, ko = (A_HEADS * A_DK) // qk_w, (2 * A_HEADS * A_DK) // v_w, (2 * A_HEADS * A_DK + A_HEADS * A_DV) // v_w
    return pl.pallas_call(
        functools.partial(_mlstm_kernel, n_chunks=n_chunks),
        grid=(n_batch, n_hp),
        in_specs=[pl.BlockSpec((rows, qk_w), lambda b, h: (b, h)),
                  pl.BlockSpec((rows, qk_w), lambda b, h: (b, kq + h)),
                  pl.BlockSpec((rows, v_w), lambda b, h: (b, kv + h)),
                  pl.BlockSpec((rows, v_w), lambda b, h: (b, ko + h)),
                  pl.BlockSpec((1, 1, n_chunks, 8, LC), lambda b, h: (h, b, 0, 0, 0)),
                  pl.BlockSpec((3, qk_w), lambda b, h: (0, h)),
                  pl.BlockSpec((3, qk_w), lambda b, h: (0, kq + h)),
                  pl.BlockSpec((1, qk_w), lambda b, h: (0, h)),
                  pl.BlockSpec((1, qk_w), lambda b, h: (0, kq + h)),
                  pl.BlockSpec((1, v_w), lambda b, h: (0, h))],
        out_specs=pl.BlockSpec((rows, v_w), lambda b, h: (b, h)),
        out_shape=jax.ShapeDtypeStruct((n_batch * rows, A_HEADS * A_DV), BF16),
        scratch_shapes=[pltpu.VMEM((rows, qk_w), BF16), pltpu.VMEM((rows, qk_w), BF16),
                        pltpu.VMEM((n_chunks, qk_w, LC), BF16), pltpu.VMEM((n_chunks, v_w, LC), BF16),
                        pltpu.VMEM((n_chunks, v_w, LC), F32), pltpu.VMEM((n_chunks, v_w, LC), F32),
                        pltpu.VMEM((4, 2 * A_DV, A_DK), F32), pltpu.VMEM((4, 8, 128), F32)],
        compiler_params=_cparams("parallel", "parallel"),
        name="mlstm_scan",
    )(p, p, p, p, g_rows, conv_w, conv_w, conv_b, conv_b, norm_g)


GLA_LEVELS = int(math.log2(LC))


def _gla_level_masks():
    j = np.arange(LC)[:, None]
    s = np.arange(LC)[None, :]
    out = []
    for fwd in (True, False):
        for lev in range(GLA_LEVELS):
            m = 1 << lev
            same = (j // (2 * m)) == (s // (2 * m))
            hi_j, hi_s = (j % (2 * m)) >= m, (s % (2 * m)) >= m
            out.append(same & (hi_j & ~hi_s if fwd else ~hi_j & hi_s))
    return np.stack(out).astype(np.float32)


def _seg_ref(b, m, fwd):
    n, w = b.shape
    r = m - 1 if fwd else m
    if 2 * m >= 8:
        b3 = b.reshape(n // (2 * m), 2 * m, w)
        return jnp.broadcast_to(b3[:, r:r + 1, :], b3.shape).reshape(n, w)
    b3 = b.reshape(n // 8, 8, w)
    sub = lax.broadcasted_iota(I32, b3.shape, 1)
    out = None
    for blk in range(8 // (2 * m)):
        row = jnp.broadcast_to(b3[:, blk * 2 * m + r:blk * 2 * m + r + 1, :], b3.shape)
        out = row if out is None else jnp.where(sub >= blk * 2 * m, row, out)
    return out.reshape(n, w)


def _gla_kernel(q_ref, k_ref, v_ref, g_ref, low_ref, cwq_ref, cwk_ref, cwv_ref, cbq_ref, cbk_ref, cbv_ref,
                gw_ref, gb_ref, ng_ref, mask_ref, y_ref, qs, ks, vs, yf, yb, st_scr, *, n_chunks):
    n_rows = n_chunks * LC
    qs[...] = (_conv_silu(q_ref, cwq_ref, cbq_ref, n_rows) * (B_DK ** -0.5)).astype(BF16)
    ks[...] = _conv_silu(k_ref, cwk_ref, cbk_ref, n_rows).astype(BF16)
    vs[...] = _conv_silu(v_ref, cwv_ref, cbv_ref, n_rows).astype(BF16)
    st_scr[...] = jnp.zeros_like(st_scr)
    tril, triu = _tri_masks()
    tril_b, triu_b = tril.astype(BF16), triu.astype(BF16)
    eye = jnp.logical_and(tril, triu).astype(F32)

    def step(cf, cb):
        dirs = (0, 1)
        fwd = (True, False)
        rows = [pl.ds(pl.multiple_of(c * LC, LC), LC) for c in (cf, cb)]
        st = [st_scr[d] for d in dirs]
        pre = [jnp.dot(low_ref[rows[d], :], gw_ref[d], precision=lax.Precision.HIGHEST,
                       preferred_element_type=F32) + gb_ref[d:d + 1, :] for d in dirs]
        lg = [_log_sigmoid(pre[d]) * (1.0 / B_TAU) for d in dirs]
        b = [_dot_exact_lhs01(tril_b if fwd[d] else triu_b, lg[d]) for d in dirs]
        q = [qs[rows[d], :].astype(F32) for d in dirs]
        k = [ks[rows[d], :].astype(F32) for d in dirs]
        v = [vs[rows[d], :] for d in dirs]
        a = [eye * jnp.sum(q[d] * k[d], axis=-1, keepdims=True) for d in dirs]
        for lev in range(GLA_LEVELS):
            for d in dirs:
                decay = jnp.exp(-jnp.abs(b[d] - _seg_ref(b[d], 1 << lev, fwd[d])))
                qt = (q[d] * decay).astype(BF16)
                kt = (k[d] * decay).astype(BF16)
                a[d] = a[d] + mask_ref[d * GLA_LEVELS + lev] * _dot_nt(qt, kt)
        intra = [_dot(a[d].astype(BF16), v[d]) for d in dirs]
        inter = [_dot_nt((q[d] * jnp.exp(b[d])).astype(BF16), st[d].astype(BF16)) for d in dirs]
        yf[rows[0], :] = intra[0] + inter[0]
        yb[rows[1], :] = intra[1] + inter[1]
        b_end = [b[0][LC - 1:LC, :], b[1][0:1, :]]
        upd = [_dot_tn(v[d], (k[d] * jnp.exp(b_end[d] - b[d])).astype(BF16)) for d in dirs]
        for d in dirs:
            st_scr[d] = st[d] * jnp.exp(b_end[d]) + upd[d]
    _scan_schedule(n_chunks, step)

    y = yf[...] + yb[...]
    yn = y * lax.rsqrt(jnp.mean(y * y, axis=-1, keepdims=True) + EPS)
    y_ref[...] = (yn * ng_ref[...] * _silu(g_ref[...].astype(F32))).astype(y_ref.dtype)


def gla_scan(p, low, conv_w, conv_b, gate_w, gate_b, norm_g, n_batch, n_chunks):
    rows = n_chunks * LC
    kq, kv, kg = B_HEADS, (2 * B_HEADS * B_DK) // B_DV, (2 * B_HEADS * B_DK + B_HEADS * B_DV) // B_DV
    masks = jnp.asarray(_gla_level_masks(), F32)
    cmap = lambda b, h: (0, 0, 0)
    return pl.pallas_call(
        functools.partial(_gla_kernel, n_chunks=n_chunks),
        grid=(n_batch, B_HEADS),
        in_specs=[pl.BlockSpec((rows, B_DK), lambda b, h: (b, h)),
                  pl.BlockSpec((rows, B_DK), lambda b, h: (b, kq + h)),
                  pl.BlockSpec((rows, B_DV), lambda b, h: (b, kv + h)),
                  pl.BlockSpec((rows, B_DV), lambda b, h: (b, kg + h)),
                  pl.BlockSpec((rows, 128), lambda b, h: (b, 0)),
                  pl.BlockSpec((3, B_DK), lambda b, h: (0, h)),
                  pl.BlockSpec((3, B_DK), lambda b, h: (0, kq + h)),
                  pl.BlockSpec((3, B_DV), lambda b, h: (0, kv + h)),
                  pl.BlockSpec((1, B_DK), lambda b, h: (0, h)),
                  pl.BlockSpec((1, B_DK), lambda b, h: (0, kq + h)),
                  pl.BlockSpec((1, B_DV), lambda b, h: (0, kv + h)),
                  pl.BlockSpec((2, 128, B_DK), lambda b, h: (0, 0, h)),
                  pl.BlockSpec((2, B_DK), lambda b, h: (0, h)),
                  pl.BlockSpec((1, B_DV), lambda b, h: (0, h)),
                  pl.BlockSpec((2 * GLA_LEVELS, LC, LC), cmap)],
        out_specs=pl.BlockSpec((rows, B_DV), lambda b, h: (b, h)),
        out_shape=jax.ShapeDtypeStruct((n_batch * rows, B_HEADS * B_DV), BF16),
        scratch_shapes=[pltpu.VMEM((rows, B_DK), BF16), pltpu.VMEM((rows, B_DK), BF16),
                        pltpu.VMEM((rows, B_DV), BF16),
                        pltpu.VMEM((rows, B_DV), F32), pltpu.VMEM((rows, B_DV), F32),
                        pltpu.VMEM((2, B_DV, B_DK), F32)],
        compiler_params=_cparams("parallel", "parallel"),
        name="gla_scan",
    )(p, p, p, p, low, conv_w, conv_w, conv_w, conv_b, conv_b, conv_b, gate_w, gate_b, norm_g, masks)


FB = 256


def _dft_mats(n_len):
    n = 2 * n_len
    nfb = n_len // FB
    f = np.arange(n_len)[:, None]
    t = np.arange(n_len)[None, :]
    ang = 2.0 * np.pi * ((f * t) % n) / n
    alt = (-1.0) ** np.arange(n_len)
    cf, sf = np.cos(ang), -np.sin(ang)
    sf[0, :] = alt
    fwd = np.stack([cf.reshape(nfb, FB, n_len), sf.reshape(nfb, FB, n_len)], axis=1).reshape(n, n_len)
    ci, si = (2.0 / n) * np.cos(ang.T), -(2.0 / n) * np.sin(ang.T)
    ci[:, 0] = 1.0 / n
    si[:, 0] = alt / n
    inv = np.stack([ci.reshape(n_len, nfb, FB), si.reshape(n_len, nfb, FB)], axis=2).reshape(n_len, n)
    return fwd.astype(np.float32), inv.astype(np.float32)


def _hy_pre_kernel(x0_ref, x1_ref, v_ref, cw0, cw1, cw2, cb0, cb1, cb2, zc_ref, zl_ref, x0c_ref, x0l_ref,
                   *, n_rows):
    x0 = _conv3(x0_ref, cw0, cb0, n_rows)
    z = _conv3(x1_ref, cw1, cb1, n_rows) * _conv3(v_ref, cw2, cb2, n_rows)
    zc_ref[...] = z[:LC].astype(BF16)
    zl_ref[...] = z[LC:].astype(BF16)
    x0c_ref[...] = x0[:LC].astype(BF16)
    x0l_ref[...] = x0[LC:].astype(BF16)


def hyena_pre(p, conv_w, conv_b, n_batch, n_chunks):
    rows = n_chunks * LC
    d = p.shape[1] // 3
    w = 256
    nj = d // w
    s_len = rows - LC
    spec = lambda k: pl.BlockSpec((rows, w), lambda b, j: (b, k * nj + j))
    cws = lambda k: pl.BlockSpec((3, w), lambda b, j: (0, k * nj + j))
    cbs = lambda k: pl.BlockSpec((1, w), lambda b, j: (0, k * nj + j))
    oc = pl.BlockSpec((LC, w), lambda b, j: (b, j))
    ol = pl.BlockSpec((s_len, w), lambda b, j: (b, j))
    return pl.pallas_call(
        functools.partial(_hy_pre_kernel, n_rows=rows),
        grid=(n_batch, nj),
        in_specs=[spec(0), spec(1), spec(2), cws(0), cws(1), cws(2), cbs(0), cbs(1), cbs(2)],
        out_specs=[oc, ol, oc, ol],
        out_shape=[jax.ShapeDtypeStruct((n_batch * LC, d), BF16), jax.ShapeDtypeStruct((n_batch * s_len, d), BF16),
                   jax.ShapeDtypeStruct((n_batch * LC, d), BF16), jax.ShapeDtypeStruct((n_batch * s_len, d), BF16)],
        compiler_params=_cparams("parallel", "parallel"),
        name="hyena_pre",
    )(p, p, p, conv_w, conv_w, conv_w, conv_b, conv_b, conv_b)


def _filt_dft_kernel(f_ref, x_ref, o_ref):
    acc = _dot(f_ref[...], x_ref[0])
    half = acc.shape[1] // 2
    o_ref[0] = acc[:, :half] + acc[:, half:]


def filter_spectrum(fwd, hsum, hdiff):
    n_len, d = hsum.shape

    def hilo(a):
        hi = a.astype(BF16)
        return jnp.concatenate([hi, (a - hi.astype(F32)).astype(BF16)], axis=1)
    xs = jnp.stack([hilo(hsum), hilo(hdiff)])
    return pl.pallas_call(
        _filt_dft_kernel,
        grid=(2, n_len // FB),
        in_specs=[pl.BlockSpec((FB, n_len), lambda w, m: (2 * m + w, 0)),
                  pl.BlockSpec((1, n_len, 2 * d), lambda w, m: (w, 0, 0))],
        out_specs=pl.BlockSpec((1, FB, d), lambda w, m: (w, m, 0)),
        out_shape=jax.ShapeDtypeStruct((2, n_len, d), F32),
        compiler_params=_cparams("parallel", "parallel"),
        name="filter_spectrum",
    )(fwd, xs)


def _dft_fwd_kernel(f_ref, z_ref, kr_ref, ki_ref, y_ref):
    acc = _dot(f_ref[...], z_ref[...])
    zr, zi = acc[:FB], acc[FB:]
    kr, ki = kr_ref[...], ki_ref[...]
    packed = jnp.logical_and(pl.program_id(1) == 0, lax.broadcasted_iota(I32, (FB, 1), 0) == 0)
    y_ref[:FB] = (zr * kr - jnp.where(packed, 0.0, zi * ki)).astype(BF16)
    y_ref[FB:] = jnp.where(packed, zi * ki, zr * ki + zi * kr).astype(BF16)


def dft_forward(fwd, z, kr, ki, n_batch):
    n_len, d = kr.shape
    nfb = n_len // FB
    return pl.pallas_call(
        _dft_fwd_kernel,
        grid=(n_batch, nfb),
        in_specs=[pl.BlockSpec((2 * FB, n_len), lambda b, m: (m, 0)),
                  pl.BlockSpec((n_len, d), lambda b, m: (b, 0)),
                  pl.BlockSpec((FB, d), lambda b, m: (m, 0)),
                  pl.BlockSpec((FB, d), lambda b, m: (m, 0))],
        out_specs=pl.BlockSpec((2 * FB, d), lambda b, m: (b * nfb + m, 0)),
        out_shape=jax.ShapeDtypeStruct((n_batch * 2 * n_len, d), BF16),
        compiler_params=_cparams("parallel", "arbitrary"),
        name="dft_forward",
    )(fwd, z, kr, ki)


def _dft_inv_kernel(ic_ref, yc_ref, zc_ref, x0c_ref, il_ref, yl_ref, zl_ref, x0l_ref, skip_ref, o_ref):
    def emit(i_ref, y_ref, z_ref, x0_ref):
        conv = _dot(i_ref[...], y_ref[...])
        o_ref[...] = (x0_ref[...].astype(F32) * (conv + z_ref[...].astype(F32) * skip_ref[...])).astype(o_ref.dtype)

    @pl.when(pl.program_id(1) == 0)
    def _():
        emit(ic_ref, yc_ref, zc_ref, x0c_ref)

    @pl.when(pl.program_id(1) > 0)
    def _():
        emit(il_ref, yl_ref, zl_ref, x0l_ref)


def dft_inverse(ctx_parts, lat_parts, skip, n_batch, tiles_per_batch):
    inv_c, yf_c, z_c, x0_c = ctx_parts
    inv_l, yf_l, z_l, x0_l = lat_parts
    d = z_l.shape[1]
    n_l = z_l.shape[0] // n_batch
    nt = n_l // TM
    lt = lambda m: jnp.maximum(m - 1, 0)
    return pl.pallas_call(
        _dft_inv_kernel,
        grid=(n_batch, tiles_per_batch),
        in_specs=[pl.BlockSpec((TM, 2 * LC), lambda b, m: (0, 0)),
                  pl.BlockSpec((2 * LC, d), lambda b, m: (b, 0)),
                  pl.BlockSpec((TM, d), lambda b, m: (b, 0)),
                  pl.BlockSpec((TM, d), lambda b, m: (b, 0)),
                  pl.BlockSpec((TM, 2 * n_l), lambda b, m: (lt(m), 0)),
                  pl.BlockSpec((2 * n_l, d), lambda b, m: (b, 0)),
                  pl.BlockSpec((TM, d), lambda b, m: (b * nt + lt(m), 0)),
                  pl.BlockSpec((TM, d), lambda b, m: (b * nt + lt(m), 0)),
                  pl.BlockSpec((1, d), lambda b, m: (0, 0))],
        out_specs=pl.BlockSpec((TM, d), lambda b, m: (b * tiles_per_batch + m, 0)),
        out_shape=jax.ShapeDtypeStruct((n_batch * tiles_per_batch * TM, d), BF16),
        compiler_params=_cparams("parallel", "arbitrary"),
        name="dft_inverse",
    )(inv_c, yf_c, z_c, x0_c, inv_l, yf_l, z_l, x0_l, skip.reshape(1, d))


def _hyena_filters(n_len, w1, b1, w_mid, b_mid, w_out, freq):
    hp = lax.Precision.HIGHEST
    d = w_out.shape[1] // 2
    n_bands = (w1.shape[0] - 1) // 2
    t = jnp.linspace(0.0, 1.0, n_len, dtype=F32)[:, None]
    pos = jnp.arange(n_len, dtype=F32)[:, None]
    bands = jnp.linspace(1e-4, n_bands - 1, n_bands, dtype=F32)[None, :]
    ang = (2.0 * math.pi / n_len) * pos * bands
    feats = jnp.concatenate([t, jnp.cos(ang), -jnp.sin(ang)], -1)
    h = jnp.sin(freq * (jnp.dot(feats, w1, precision=hp) + b1))
    for m in range(w_mid.shape[0]):
        h = jnp.sin(freq * (jnp.dot(h, w_mid[m], precision=hp) + b_mid[m]))
    h = jnp.dot(h, w_out, precision=hp).reshape(n_len, 2, d)
    deltas = jnp.abs(jnp.linspace(math.log(1e-2) / 1.5, math.log(1e-2) / 0.3, d, dtype=F32))
    h = h * jnp.exp(-t * deltas)[:, None, :]
    hf, hb = h[:, 0], h[:, 1]
    l1 = jnp.sum(jnp.abs(hf), 0) + jnp.sum(jnp.abs(hb[1:]), 0)
    return hf / l1, hb / l1


def hyena_mix(p, conv_w, conv_b, filt, skip, n_batch, n_chunks):
    rows = n_chunks * LC
    zc, zl, x0c, x0l = hyena_pre(p, conv_w, conv_b, n_batch, n_chunks)
    parts = []
    for z, x0 in ((zc, x0c), (zl, x0l)):
        n_len = z.shape[0] // n_batch
        fwd_np, inv_np = _dft_mats(n_len)
        fwd, inv = jnp.asarray(fwd_np, F32).astype(BF16), jnp.asarray(inv_np, F32).astype(BF16)
        hf, hb = _hyena_filters(n_len, *filt)
        hb0 = hb.at[0].set(0.0)
        hsum, hdiff = hf + hb0, hf - hb0
        spec = filter_spectrum(fwd, hsum, hdiff)
        alt = jnp.asarray((-1.0) ** np.arange(n_len), F32)[:, None]
        kr, ki = spec[0], spec[1].at[0].set(jnp.sum(alt * hsum, axis=0))
        parts.append((inv, dft_forward(fwd, z, kr, ki, n_batch), z, x0))
    return dft_inverse(parts[0], parts[1], skip, n_batch, n_chunks)


P_DTYPE = BF16
N_MAIN = 3072
N_EXTRA = 128


def _pad_cols(w, n):
    return jnp.pad(w, ((0, 0), (0, n - w.shape[1])))


def _scan_order_kernel(x_ref, o_ref, *, ctx_len, rows_per_col):
    o_ref[0:ctx_len, :] = x_ref[0:ctx_len, :]
    for w in range(GRID_W):
        o_ref[ctx_len + w * rows_per_col:ctx_len + (w + 1) * rows_per_col, :] = (
            x_ref[pl.ds(ctx_len + w, rows_per_col, stride=GRID_W), :])


def to_scan_order_f32(a, n_batch, ctx_len):
    t, d = a.shape
    rows = t // n_batch
    cw = LANES
    return pl.pallas_call(
        functools.partial(_scan_order_kernel, ctx_len=ctx_len, rows_per_col=(rows - ctx_len) // GRID_W),
        grid=(n_batch, d // cw),
        in_specs=[pl.BlockSpec((rows, cw), lambda b, j: (b, j))],
        out_specs=pl.BlockSpec((rows, cw), lambda b, j: (b, j)),
        out_shape=jax.ShapeDtypeStruct(a.shape, a.dtype),
        compiler_params=_cparams("parallel", "parallel"),
        name="to_scan_order",
    )(a)


def _from_scan_order(a, n_batch, ctx_len):
    w = a.shape[1]
    a3 = a.reshape(n_batch, -1, w)
    lat = a3[:, ctx_len:]
    s_len = lat.shape[1]
    lat = lat.reshape(n_batch, GRID_W, s_len // GRID_W, w).transpose(0, 2, 1, 3).reshape(n_batch, s_len, w)
    return jnp.concatenate([a3[:, :ctx_len], lat], axis=1).reshape(-1, w)


def kernel(x, c, ctx, c_ctx, ada_w, ada_b, norm1_g, norm2_g, final_g, ml_w_in, ml_b_gate, ml_conv_w, ml_conv_b, ml_norm_g, ml_w_out, gla_w_in, gla_conv_w, gla_conv_b, gla_gate_w2, gla_gate_b, gla_norm_g, gla_w_out, hy_w_in, hy_b_in, hy_conv_w, hy_conv_b, hy_filt_w1, hy_filt_b1, hy_filt_w_mid, hy_filt_b_mid, hy_filt_w_out, hy_filt_freq, hy_skip, hy_w_out, hy_b_out, router_w, router_b, exp_w1, exp_w3, exp_w2):
    n_batch, s_len, d = x.shape
    ctx_len = ctx.shape[1]
    depth = ada_w.shape[0]
    assert ctx_len == LC == TM and s_len % LC == 0 and s_len % GRID_W == 0
    n_chunks = (ctx_len + s_len) // LC
    t_all = n_batch * n_chunks * LC
    lat_tiles = s_len // TM

    mod_rows = jnp.zeros((16, d), F32).at[0].set(c_ctx).at[1:1 + n_batch].set(c)
    mods = ada_mods(mod_rows, ada_w, ada_b).reshape(depth, 16, 6, d)
    xs = jnp.concatenate([ctx, x], axis=1).reshape(t_all, d)
    router_wt = router_w.T
    zeros_d = jnp.zeros((d,), F32)
    out = None

    for i in range(depth):
        last = i == depth - 1
        kind, j = i % 3, i // 3
        if kind == 0:
            w_in = _pad_cols(ml_w_in[j], N_MAIN + N_EXTRA).astype(BF16)
            bias = jnp.zeros((N_MAIN + N_EXTRA,), F32).at[N_MAIN:N_MAIN + 4 * A_HEADS].set(ml_b_gate[j])
            p, pg = in_proj(xs, mods[i], norm1_g[i], w_in, bias, N_MAIN, n_chunks, P_DTYPE)
            g4 = pg[:, :4 * A_HEADS].reshape(t_all, 4, A_HEADS // 2, 2)
            g_cols = g4.transpose(2, 0, 1, 3).reshape(A_HEADS // 2, t_all, 8)
            g_rows = g_cols.reshape(A_HEADS // 2, n_batch, n_chunks, LC, 8).transpose(0, 1, 2, 4, 3)
            y = mlstm_scan(p, g_rows, ml_conv_w[j], ml_conv_b[j].reshape(1, -1),
                           ml_norm_g[j].reshape(1, -1), n_batch, n_chunks)
            w_out, b_out = ml_w_out[j], zeros_d
        elif kind == 1:
            w_in = _pad_cols(gla_w_in[j], N_MAIN + N_EXTRA).astype(BF16)
            bias = jnp.zeros((N_MAIN + N_EXTRA,), F32)
            p, pg = in_proj(to_scan_order_f32(xs, n_batch, ctx_len), mods[i], norm1_g[i], w_in, bias, N_MAIN,
                            n_chunks, P_DTYPE)
            gate_w = jnp.zeros((2, 128, B_HEADS * B_DK), F32)
            gate_w = gate_w.at[0, :B_RANK].set(gla_gate_w2[j, 0]).at[1, B_RANK:2 * B_RANK].set(gla_gate_w2[j, 1])
            y = gla_scan(p, pg, gla_conv_w[j], gla_conv_b[j].reshape(1, -1), gate_w, gla_gate_b[j],
                         gla_norm_g[j].reshape(1, -1), n_batch, n_chunks)
            y = _from_scan_order(y, n_batch, ctx_len)
            w_out, b_out = gla_w_out[j], zeros_d
        else:
            (p,) = in_proj(xs, mods[i], norm1_g[i], hy_w_in[j].astype(BF16), hy_b_in[j], N_MAIN, n_chunks, P_DTYPE)
            filt = (hy_filt_w1[j], hy_filt_b1[j], hy_filt_w_mid[j], hy_filt_b_mid[j], hy_filt_w_out[j],
                    hy_filt_freq[j])
            y = hyena_mix(p, hy_conv_w[j], hy_conv_b[j].reshape(1, -1), filt, hy_skip[j], n_batch, n_chunks)
            w_out, b_out = hy_w_out[j], hy_b_out[j]

        x2, h2, ids, gates, ranks, counts = out_route(
            xs, y, mods[i], w_out.astype(BF16), b_out, norm2_g[i], router_wt, router_b, n_chunks, 1, last)
        n_tok = x2.shape[0]
        dest_tiles, n_slots, block_exp, n_used, pad_end = moe_plan(ids, ranks, counts, n_tok)
        xb = moe_dispatch(h2, dest_tiles, pad_end, n_used, n_slots)
        yb = moe_experts(xb, block_exp, n_used, exp_w1, exp_w3, exp_w2, i)
        if last:
            out = moe_combine(x2, yb, dest_tiles, gates.T, mods[i], lambda t: 1 + t // lat_tiles, final_g)
        else:
            xs = moe_combine(x2, yb, dest_tiles, gates.T, mods[i],
                             lambda t: jnp.where(t % n_chunks == 0, 0, 1 + t // n_chunks))
    return out.reshape(n_batch, s_len, d)
```

```python
import functools
import math

import numpy as np
import jax
import jax.numpy as jnp
from jax import lax
from jax.experimental import pallas as pl
from jax.experimental.pallas import tpu as pltpu

F32 = jnp.float32
BF16 = jnp.bfloat16
I32 = jnp.int32

EPS = 1e-6
TM = 256
LC = 256
MOE_BLOCK = 512
N_EXPERTS = 16
N_GROUPS = 4
EPG = N_EXPERTS // N_GROUPS
A_HEADS, A_DK, A_DV = 8, 64, 128
B_HEADS, B_DK, B_DV = 4, 128, 256
B_RANK = 16
B_TAU = 16.0
GRID_W = 64
VMEM_LIMIT_V7X = 56 * 1024 * 1024


def _cparams(*sem):
    return pltpu.CompilerParams(dimension_semantics=sem, vmem_limit_bytes=VMEM_LIMIT_V7X)


def _silu(x):
    return x * (1.0 / (1.0 + jnp.exp(-x)))


def _sigmoid(x):
    return 1.0 / (1.0 + jnp.exp(-x))


def _log_sigmoid(x):
    return jnp.minimum(x, 0.0) - jnp.log(1.0 + jnp.exp(-jnp.abs(x)))


def _split3(x):
    hi = x.astype(BF16)
    r1 = x - hi.astype(F32)
    mid = r1.astype(BF16)
    lo = (r1 - mid.astype(F32)).astype(BF16)
    return hi, mid, lo


def _dot(a, b):
    return jnp.dot(a, b, preferred_element_type=F32)


def _dot_nt(a, b):
    return lax.dot_general(a, b, (((1,), (1,)), ((), ())), preferred_element_type=F32)


def _dot_tn(a, b):
    return lax.dot_general(a, b, (((0,), (0,)), ((), ())), preferred_element_type=F32)


LANES = 128
SUBLANES = 8


def _tok_tiles_store(ref, base, x):
    n, d = x.shape
    per = d // LANES
    for c in range(per):
        ref[pl.ds(base + c, n, stride=per), :] = x[:, c * LANES:(c + 1) * LANES]


def _tok_tiles_load(ref, base, n, d):
    per = d // LANES
    return jnp.concatenate([ref[pl.ds(base + c, n, stride=per), :] for c in range(per)], axis=1)


def _dot_exact_lhs01(m01, x):
    hi, mid, lo = _split3(x)
    return (_dot(m01, hi) + _dot(m01, mid)) + _dot(m01, lo)


def _dot_exact_rhs01(x, m01):
    hi, mid, lo = _split3(x)
    return (_dot(hi, m01) + _dot(mid, m01)) + _dot(lo, m01)


def _ada_kernel(s_ref, w_ref, b_ref, o_ref):
    s = _silu(s_ref[...]).astype(BF16)
    o_ref[0] = _dot(s, w_ref[0].astype(BF16)) + b_ref[0]


def ada_mods(rows, ada_w, ada_b):
    depth, d, n6 = ada_w.shape
    nr = rows.shape[0]
    tn = 1536
    return pl.pallas_call(
        _ada_kernel,
        grid=(depth, n6 // tn),
        in_specs=[pl.BlockSpec((nr, d), lambda l, j: (0, 0)),
                  pl.BlockSpec((1, d, tn), lambda l, j: (l, 0, j)),
                  pl.BlockSpec((1, 1, tn), lambda l, j: (l, 0, j))],
        out_specs=pl.BlockSpec((1, nr, tn), lambda l, j: (l, 0, j)),
        out_shape=jax.ShapeDtypeStruct((depth, nr, n6), F32),
        compiler_params=_cparams("parallel", "parallel"),
        name="ada_mods",
    )(rows, ada_w, ada_b.reshape(depth, 1, n6))


def _norm_mod(x, g, shift, scale):
    y = x * lax.rsqrt(jnp.mean(x * x, axis=-1, keepdims=True) + EPS) * g
    return y * (1.0 + scale) + shift


def _in_proj_kernel(x_ref, mod_ref, g_ref, w_ref, b_ref, p_ref, *rest, n_main):
    h = _norm_mod(x_ref[...], g_ref[...], mod_ref[0, 0:1, :], mod_ref[0, 1:2, :]).astype(BF16)
    p_ref[...] = (_dot(h, w_ref[:, :n_main]) + b_ref[:, :n_main]).astype(p_ref.dtype)
    if rest:
        rest[0][...] = _dot(h, w_ref[:, n_main:]) + b_ref[:, n_main:]


def in_proj(x, mods_l, g, w, b, n_main, tiles_per_batch, p_dtype):
    t, d = x.shape
    n = w.shape[1]
    n_extra = n - n_main
    mod_row = lambda i: jnp.where(i % tiles_per_batch == 0, 0, 1 + i // tiles_per_batch)
    out_shape = [jax.ShapeDtypeStruct((t, n_main), p_dtype)]
    out_specs = [pl.BlockSpec((TM, n_main), lambda i: (i, 0))]
    if n_extra:
        out_shape.append(jax.ShapeDtypeStruct((t, n_extra), F32))
        out_specs.append(pl.BlockSpec((TM, n_extra), lambda i: (i, 0)))
    return pl.pallas_call(
        functools.partial(_in_proj_kernel, n_main=n_main),
        grid=(t // TM,),
        in_specs=[pl.BlockSpec((TM, d), lambda i: (i, 0)),
                  pl.BlockSpec((1, 6, d), lambda i: (mod_row(i), 0, 0)),
                  pl.BlockSpec((1, d), lambda i: (0, 0)),
                  pl.BlockSpec((d, n), lambda i: (0, 0)),
                  pl.BlockSpec((1, n), lambda i: (0, 0))],
        out_specs=out_specs,
        out_shape=out_shape,
        compiler_params=_cparams("parallel"),
        name="in_proj",
    )(x, mods_l, g.reshape(1, d), w, b.reshape(1, n))


def _first_argmax(vals):
    best, idx = vals[0], jnp.zeros(vals[0].shape, I32)
    for i in range(1, len(vals)):
        take = vals[i] > best
        best = jnp.where(take, vals[i], best)
        idx = jnp.where(take, i, idx)
    return best, idx


def _select_row(rows, idx):
    out = rows[0]
    for i in range(1, len(rows)):
        out = jnp.where(idx == i, rows[i], out)
    return out


def _route(scores, sel):
    neg = jnp.float32(-jnp.inf)
    srow = [sel[e:e + 1, :] for e in range(N_EXPERTS)]
    crow = [scores[e:e + 1, :] for e in range(N_EXPERTS)]
    gscore = []
    for g in range(N_GROUPS):
        a = srow[g * EPG:(g + 1) * EPG]
        m1, i1 = _first_argmax(a)
        m2, _ = _first_argmax([jnp.where(i1 == i, neg, a[i]) for i in range(EPG)])
        gscore.append(m1 + m2)
    _, grp = _first_argmax(gscore)
    in_sel = [_select_row([srow[g * EPG + i] for g in range(N_GROUPS)], grp) for i in range(EPG)]
    in_sc = [_select_row([crow[g * EPG + i] for g in range(N_GROUPS)], grp) for i in range(EPG)]
    _, l1 = _first_argmax(in_sel)
    _, l2 = _first_argmax([jnp.where(l1 == i, neg, in_sel[i]) for i in range(EPG)])
    g1 = _select_row(in_sc, l1)
    g2 = _select_row(in_sc, l2)
    tot = g1 + g2
    ids = jnp.concatenate([grp * EPG + l1, grp * EPG + l2], axis=0)
    gates = jnp.concatenate([g1 / tot, g2 / tot], axis=0)
    return ids, gates


ROUTE_SUB = 4


def _out_route_kernel(*refs, mod_row):
    x_refs, y_refs = refs[:ROUTE_SUB], refs[ROUTE_SUB:2 * ROUTE_SUB]
    (mod_ref, w_ref, b_ref, g_ref, rw_ref, rb_ref,
     xo_ref, h_ref, ids_ref, gates_ref, rank_ref, cnt_ref, carry_ref) = refs[2 * ROUTE_SUB:]
    i = pl.program_id(0)

    @pl.when(i == 0)
    def _():
        carry_ref[...] = jnp.zeros_like(carry_ref)

    halves = tuple(range(ROUTE_SUB))
    rows = [slice(k * TM, (k + 1) * TM) for k in halves]
    mod = [mod_ref[mod_row(ROUTE_SUB * i + k)] for k in halves]
    proj = [_dot(y_ref[...], w_ref[...]) for y_ref in y_refs]
    xn = [x_ref[...] + mod[k][2:3, :] * (proj[k] + b_ref[...]) for k, x_ref in enumerate(x_refs)]
    h = [_norm_mod(xn[k], g_ref[...], mod[k][3:4, :], mod[k][4:5, :]) for k in halves]
    rw = rw_ref[...]
    rw_hi = rw.astype(BF16)
    rw_2 = jnp.concatenate([rw_hi, (rw - rw_hi.astype(F32)).astype(BF16)], axis=0)
    h_hi = [h[k].astype(BF16) for k in halves]
    h_lo = [(h[k] - h_hi[k].astype(F32)).astype(BF16) for k in halves]
    l_1 = [_dot_nt(rw_2, h_hi[k]) for k in halves]
    l_2 = [_dot_nt(rw_hi, h_lo[k]) for k in halves]
    logits = [(l_1[k][:N_EXPERTS] + l_1[k][N_EXPERTS:]) + l_2[k] for k in halves]
    for k in halves:
        xo_ref[rows[k], :] = xn[k]
        _tok_tiles_store(h_ref, k * TM * SUBLANES, h[k])
    scores = [_sigmoid(logits[k]) for k in halves]
    routes = [_route(scores[k], scores[k] + rb_ref[:, 0:1]) for k in halves]
    erow = lax.broadcasted_iota(I32, (N_EXPERTS, TM), 0)
    before = (lax.broadcasted_iota(I32, (TM, TM), 0) < lax.broadcasted_iota(I32, (TM, TM), 1)).astype(BF16)
    oh = [[(erow == routes[k][0][j:j + 1, :]).astype(F32) for j in range(2)] for k in halves]
    pre = [[_dot(oh[k][j].astype(BF16), before) for j in range(2)] for k in halves]
    tot = [[jnp.sum(oh[k][j], axis=1, keepdims=True) for j in range(2)] for k in halves]

    carry = carry_ref[:, 0:1]
    for k in halves:
        ids_ref[:, rows[k]] = routes[k][0]
        gates_ref[:, rows[k]] = routes[k][1]
        r0 = jnp.sum(oh[k][0] * (carry + pre[k][0]), axis=0, keepdims=True)
        r1 = jnp.sum(oh[k][1] * (carry + tot[k][0] + pre[k][1]), axis=0, keepdims=True)
        rank_ref[:, rows[k]] = jnp.concatenate([r0, r1], axis=0).astype(I32)
        carry = carry + tot[k][0] + tot[k][1]
    carry_ref[...] = jnp.broadcast_to(carry, carry_ref.shape)
    cnt_ref[...] = jnp.broadcast_to(carry, cnt_ref.shape).astype(I32)


def out_route(x, y, mods_l, w_out, b_out, g2n, router_wt, router_b, tiles_per_batch, ctx_tiles, skip_ctx):
    t, d = x.shape
    nb = t // (tiles_per_batch * TM)
    if skip_ctx:
        per = tiles_per_batch - ctx_tiles
        src = lambda i: (i // per) * tiles_per_batch + ctx_tiles + i % per
        mod_row = lambda i: 1 + i // per
        n_tiles = nb * per
    else:
        src = lambda i: i
        mod_row = lambda i: jnp.where(i % tiles_per_batch < ctx_tiles, 0, 1 + i // tiles_per_batch)
        n_tiles = nb * tiles_per_batch
    ns = ROUTE_SUB
    assert n_tiles % ns == 0 and d == SUBLANES * LANES
    tr = n_tiles * TM
    tok = lambda i: (i, 0)
    row2 = lambda i: (0, i)
    tile_specs = [pl.BlockSpec((TM, d), functools.partial(lambda i, k: (src(ns * i + k), 0), k=k))
                  for k in range(ns)]
    outs = pl.pallas_call(
        functools.partial(_out_route_kernel, mod_row=mod_row),
        grid=(n_tiles // ns,),
        in_specs=tile_specs + tile_specs + [
            pl.BlockSpec(mods_l.shape, lambda i: (0, 0, 0)),
            pl.BlockSpec((d, d), lambda i: (0, 0)),
            pl.BlockSpec((1, d), lambda i: (0, 0)),
            pl.BlockSpec((1, d), lambda i: (0, 0)),
            pl.BlockSpec((N_EXPERTS, d), lambda i: (0, 0)),
            pl.BlockSpec((N_EXPERTS, 128), lambda i: (0, 0))],
        out_specs=[pl.BlockSpec((ns * TM, d), tok),
                   pl.BlockSpec((ns * TM * SUBLANES, LANES), tok),
                   pl.BlockSpec((2, ns * TM), row2),
                   pl.BlockSpec((2, ns * TM), row2),
                   pl.BlockSpec((2, ns * TM), row2),
                   pl.BlockSpec((N_EXPERTS, 128), lambda i: (0, 0))],
        out_shape=[jax.ShapeDtypeStruct((tr, d), F32),
                   jax.ShapeDtypeStruct((tr * SUBLANES, LANES), F32),
                   jax.ShapeDtypeStruct((2, tr), I32),
                   jax.ShapeDtypeStruct((2, tr), F32),
                   jax.ShapeDtypeStruct((2, tr), I32),
                   jax.ShapeDtypeStruct((N_EXPERTS, 128), I32)],
        scratch_shapes=[pltpu.VMEM((N_EXPERTS, 128), F32)],
        compiler_params=_cparams("arbitrary"),
        name="out_route",
    )(*([x] * ns + [y] * ns), mods_l, w_out, b_out.reshape(1, d), g2n.reshape(1, d), router_wt,
      jnp.broadcast_to(router_b.reshape(N_EXPERTS, 1), (N_EXPERTS, 128)))
    return outs


def _moe_kernel(bexp_ref, nused_ref, x_ref, w1_ref, w3_ref, w2_ref, o_ref, w1b, w3b, w2b):
    i = pl.program_id(0)
    prev = bexp_ref[jnp.maximum(i - 1, 0)]
    fresh = jnp.logical_or(i == 0, bexp_ref[i] != prev)
    used = i < nused_ref[0]

    @pl.when(jnp.logical_and(fresh, used))
    def _():
        w1b[...] = w1_ref[0].astype(BF16)
        w3b[...] = w3_ref[0].astype(BF16)
        w2b[...] = w2_ref[0].astype(BF16)

    @pl.when(used)
    def _():
        x = _tok_tiles_load(x_ref, 0, MOE_BLOCK, w1b.shape[0]).astype(BF16)
        a = _dot(x, w1b[...])
        b = _dot(x, w3b[...])
        hmid = (_silu(a) * b).astype(BF16)
        _tok_tiles_store(o_ref, 0, _dot(hmid, w2b[...]))

    @pl.when(jnp.logical_not(used))
    def _():
        o_ref[...] = jnp.zeros_like(o_ref)


def moe_experts(xb, block_exp, n_used, w1, w3, w2, layer):
    d, ff = w1.shape[2], w1.shape[3]
    per = d // LANES
    n_slots, dh = xb.shape[0] // per, LANES
    n_blocks = n_slots // MOE_BLOCK
    wmap = lambda i, be, nu: (layer, be[i], 0, 0)
    return pl.pallas_call(
        _moe_kernel,
        grid_spec=pltpu.PrefetchScalarGridSpec(
            num_scalar_prefetch=2,
            grid=(n_blocks,),
            in_specs=[pl.BlockSpec((MOE_BLOCK * per, dh), lambda i, be, nu: (i, 0)),
                      pl.BlockSpec((None, 1, d, ff), wmap),
                      pl.BlockSpec((None, 1, d, ff), wmap),
                      pl.BlockSpec((None, 1, ff, d), wmap)],
            out_specs=pl.BlockSpec((MOE_BLOCK * per, dh), lambda i, be, nu: (i, 0)),
            scratch_shapes=[pltpu.VMEM((d, ff), BF16), pltpu.VMEM((d, ff), BF16),
                            pltpu.VMEM((ff, d), BF16)]),
        out_shape=jax.ShapeDtypeStruct((n_slots * per, dh), F32),
        compiler_params=_cparams("arbitrary"),
        name="moe_experts",
    )(block_exp, n_used, xb, w1, w3, w2)


def moe_plan(ids, ranks, counts, n_tokens):
    n_assign = 2 * n_tokens
    n_blocks = n_assign // MOE_BLOCK + N_EXPERTS
    cnt = counts[:, 0]
    padded = (cnt + MOE_BLOCK - 1) // MOE_BLOCK * MOE_BLOCK
    pad_end = jnp.cumsum(padded)
    pad_start = pad_end - padded
    start_of = sum(jnp.where(ids == e, pad_start[e], 0) for e in range(N_EXPERTS))
    dest = start_of + ranks
    blk_start = jnp.arange(n_blocks, dtype=I32) * MOE_BLOCK
    block_exp = jnp.minimum(jnp.sum(blk_start[:, None] >= pad_end[None, :], axis=1), N_EXPERTS - 1).astype(I32)
    n_used = (pad_end[-1] // MOE_BLOCK).astype(I32).reshape(1)
    dest_tiles = dest.reshape(2, n_tokens // TM, TM).transpose(1, 0, 2)
    return dest_tiles, n_blocks * MOE_BLOCK, block_exp, n_used, pad_end.astype(I32)


def _tile_of(idx):
    return pl.ds(pl.multiple_of(idx * SUBLANES, SUBLANES), SUBLANES)


def _token_copies(src_of, dst_of, dest_ref, sem):
    def body(r, carry):
        for k in range(2):
            pltpu.make_async_copy(src_of(k, r, dest_ref[0, k, r]), dst_of(k, r, dest_ref[0, k, r]),
                                  sem).start(priority=k)
        return carry
    lax.fori_loop(0, TM, body, 0, unroll=8)


def _dispatch_kernel(pad_end_ref, nused_ref, dest_ref, h_ref, xb_ref, zeros, ring, sem, zsem, *, n_blocks):
    i = pl.program_id(0)
    n = pl.num_programs(0)
    blk_rows = MOE_BLOCK * SUBLANES

    def zero_block(blk):
        return pltpu.make_async_copy(zeros, xb_ref.at[pl.ds(pl.multiple_of(blk * blk_rows, blk_rows), blk_rows)],
                                     zsem)

    @pl.when(i == 0)
    def _():
        zeros[...] = jnp.zeros_like(zeros)
        fills = []
        for e in range(N_EXPERTS):
            end = pad_end_ref[e]
            start = pad_end_ref[e - 1] if e else 0
            fills.append((end > start, end // MOE_BLOCK - 1))
        for j in range(N_EXPERTS):
            fills.append((nused_ref[0] + j < n_blocks, nused_ref[0] + j))
        for cond, blk in fills:
            @pl.when(cond)
            def _():
                zero_block(blk).start()
        for cond, blk in fills:
            @pl.when(cond)
            def _():
                zero_block(blk).wait()

    def tile_wait(slot):
        for _ in range(2):
            pltpu.make_async_copy(h_ref, xb_ref.at[pl.ds(0, TM * SUBLANES)], sem.at[slot]).wait()

    slot = i % 2

    @pl.when(i >= 2)
    def _():
        tile_wait(slot)
    ring[slot] = h_ref[...]
    _token_copies(lambda k, r, s: ring.at[slot, _tile_of(r)], lambda k, r, s: xb_ref.at[_tile_of(s)], dest_ref,
                  sem.at[slot])

    @pl.when(i == n - 1)
    def _():
        @pl.when(n >= 2)
        def _():
            tile_wait(1 - slot)
        tile_wait(slot)


def moe_dispatch(h, dest_tiles, pad_end, n_used, n_slots):
    rows, dh = h.shape
    return pl.pallas_call(
        functools.partial(_dispatch_kernel, n_blocks=n_slots // MOE_BLOCK),
        grid_spec=pltpu.PrefetchScalarGridSpec(
            num_scalar_prefetch=2,
            grid=(rows // (TM * SUBLANES),),
            in_specs=[pl.BlockSpec((1, 2, TM), lambda i, pe, nu: (i, 0, 0), memory_space=pltpu.SMEM),
                      pl.BlockSpec((TM * SUBLANES, dh), lambda i, pe, nu: (i, 0))],
            out_specs=pl.BlockSpec(memory_space=pl.ANY),
            scratch_shapes=[pltpu.VMEM((MOE_BLOCK * SUBLANES, dh), h.dtype),
                            pltpu.VMEM((2, TM * SUBLANES, dh), h.dtype),
                            pltpu.SemaphoreType.DMA((2,)), pltpu.SemaphoreType.DMA(())]),
        out_shape=jax.ShapeDtypeStruct((n_slots * SUBLANES, dh), h.dtype),
        compiler_params=_cparams("arbitrary"),
        name="moe_dispatch",
    )(pad_end, n_used, dest_tiles, h)


def _combine_kernel(dcur_ref, dnext_ref, x_ref, yb_ref, gt_ref, mod_ref, *rest, final):
    buf, sem = rest[-2:]
    i = pl.program_id(0)
    n = pl.num_programs(0)

    def fetch(dest_ref, slot):
        _token_copies(lambda k, r, s: yb_ref.at[_tile_of(s)], lambda k, r, s: buf.at[slot, k, _tile_of(r)],
                      dest_ref, sem.at[slot])

    @pl.when(i == 0)
    def _():
        fetch(dcur_ref, 0)

    @pl.when(i + 1 < n)
    def _():
        fetch(dnext_ref, (i + 1) % 2)

    slot = i % 2
    for k in range(2):
        pltpu.make_async_copy(yb_ref.at[pl.ds(0, TM * SUBLANES)], buf.at[slot, k], sem.at[slot]).wait()
    d = x_ref.shape[1]
    moe = (gt_ref[:, 0:1] * _tok_tiles_load(buf.at[slot, 0], 0, TM, d)
           + gt_ref[:, 1:2] * _tok_tiles_load(buf.at[slot, 1], 0, TM, d))
    xn = x_ref[...] + mod_ref[0, 5:6, :] * moe
    if final:
        g_ref, o_ref = rest[:2]
        o_ref[...] = xn * lax.rsqrt(jnp.mean(xn * xn, axis=-1, keepdims=True) + EPS) * g_ref[...]
    else:
        rest[0][...] = xn


def moe_combine(x, yb, dest_tiles, gates_t, mods_l, mod_row, final_g=None):
    t, d = x.shape
    n_tiles = t // TM
    tok = lambda i: (i, 0)
    in_specs = [pl.BlockSpec((1, 2, TM), lambda i: (i, 0, 0), memory_space=pltpu.SMEM),
                pl.BlockSpec((1, 2, TM), lambda i: (jnp.minimum(i + 1, n_tiles - 1), 0, 0), memory_space=pltpu.SMEM),
                pl.BlockSpec((TM, d), tok),
                pl.BlockSpec(memory_space=pl.ANY),
                pl.BlockSpec((TM, 2), tok),
                pl.BlockSpec((1, 6, d), lambda i: (mod_row(i), 0, 0))]
    args = [dest_tiles, dest_tiles, x, yb, gates_t, mods_l]
    if final_g is not None:
        in_specs.append(pl.BlockSpec((1, d), lambda i: (0, 0)))
        args.append(final_g.reshape(1, d))
    return pl.pallas_call(
        functools.partial(_combine_kernel, final=final_g is not None),
        grid=(n_tiles,),
        in_specs=in_specs,
        out_specs=pl.BlockSpec((TM, d), tok),
        out_shape=jax.ShapeDtypeStruct((t, d), F32),
        scratch_shapes=[pltpu.VMEM((2, 2, TM * SUBLANES, LANES), F32), pltpu.SemaphoreType.DMA((2,))],
        compiler_params=_cparams("arbitrary"),
        name="moe_combine",
    )(*args)


def _conv3(x_ref, cw_ref, cb_ref, n_rows):
    x = x_ref[...].astype(F32)
    row = lax.broadcasted_iota(I32, (n_rows, 1), 0)
    first = jnp.logical_or(row == 0, row == LC)
    last = jnp.logical_or(row == LC - 1, row == n_rows - 1)
    prev = jnp.where(first, 0.0, pltpu.roll(x, 1, 0))
    nxt = jnp.where(last, 0.0, pltpu.roll(x, n_rows - 1, 0))
    return prev * cw_ref[0:1, :] + x * cw_ref[1:2, :] + nxt * cw_ref[2:3, :] + cb_ref[...]


def _conv_silu(x_ref, cw_ref, cb_ref, n_rows):
    return _silu(_conv3(x_ref, cw_ref, cb_ref, n_rows))


def _tri_masks():
    ri = lax.broadcasted_iota(I32, (LC, LC), 0)
    ci = lax.broadcasted_iota(I32, (LC, LC), 1)
    return ri >= ci, ri <= ci


def _scan_schedule(n_chunks, step):
    step(0, 0)

    def body(t, carry):
        step(t, n_chunks - t)
        return carry
    lax.fori_loop(1, n_chunks, body, 0)


def _rows16(rows):
    n = rows[0].shape[1]
    ri = lax.broadcasted_iota(I32, (16, n), 0)
    out = jnp.zeros((16, n), F32)
    for i, r in enumerate(rows):
        out = jnp.where(ri == i, jnp.broadcast_to(r, (16, n)), out)
    return out.astype(BF16)


def _mlstm_kernel(q_ref, k_ref, v_ref, o_ref, gt_ref, cwq_ref, cwk_ref, cbq_ref, cbk_ref, ng_ref,
                  y_ref, qs, ks, kst, vt, yft, ybt, ct_scr, m_scr, *, n_chunks):
    n_rows = n_chunks * LC
    qs[...] = (_conv_silu(q_ref, cwq_ref, cbq_ref, n_rows) * (A_DK ** -0.5)).astype(BF16)
    kf = _conv_silu(k_ref, cwk_ref, cbk_ref, n_rows)
    ks[...] = kf.astype(BF16)
    for c in range(n_chunks):
        kst[c] = kf[c * LC:(c + 1) * LC, :].T.astype(BF16)
        vt[c] = v_ref[c * LC:(c + 1) * LC, :].astype(F32).T.astype(BF16)
    ct_scr[...] = jnp.zeros_like(ct_scr)
    m_scr[...] = jnp.zeros_like(m_scr)
    tril, triu = _tri_masks()
    tril_b, triu_b = tril.astype(BF16), triu.astype(BF16)
    ones_rows = (lax.broadcasted_iota(I32, (A_DV, LC), 0) == 0).astype(BF16)
    one = jnp.ones((1, LC), F32)
    neg = jnp.float32(-jnp.inf)

    f32 = lambda t: t.astype(F32)

    def step(cf, cb):
        chains = [(cf, True, 0), (cf, True, 1), (cb, False, 0), (cb, False, 1)]
        m_prev = [m_scr[i, 0:1, 0:1] for i in range(4)]
        ct = [ct_scr[i] for i in range(4)]
        gts = {True: gt_ref[0, 0, cf], False: gt_ref[0, 0, cb]}
        css = {True: _dot_exact_rhs01(_log_sigmoid(gts[True]), triu_b),
               False: _dot_exact_rhs01(_log_sigmoid(gts[False]), tril_b)}
        rows = [pl.ds(pl.multiple_of(c * LC, LC), LC) for c, _, _ in chains]
        qh = [qs[rows[i], hh * A_DK:(hh + 1) * A_DK] for i, (_, _, hh) in enumerate(chains)]
        kh = [ks[rows[i], hh * A_DK:(hh + 1) * A_DK] for i, (_, _, hh) in enumerate(chains)]
        qk_t = [_dot_nt(kh[i], qh[i]) for i in range(4)]
        fcum, g, log_d = [], [], []
        for c, fwd, hh in chains:
            d = 0 if fwd else 1
            fc_row = css[fwd][4 * d + 2 + hh:4 * d + 3 + hh, :]
            g_row = gts[fwd][4 * d + hh:4 * d + hh + 1, :] - fc_row
            f_hi, f_mid, f_lo = _split3(fc_row)
            g_hi, g_mid, g_lo = _split3(g_row)
            f_slab = _rows16([f32(f_hi), f32(f_mid), f32(f_lo), one, one, one])
            g_slab = _rows16([one, one, one, f32(g_hi), f32(g_mid), f32(g_lo)])
            log_d.append(jnp.where(triu if fwd else tril, _dot_tn(g_slab, f_slab), neg))
            fcum.append(fc_row)
            g.append(g_row)
        li = [fcum[i] + m_prev[i] for i in range(4)]
        m_row = [jnp.maximum(li[i], jnp.max(log_d[i], axis=0, keepdims=True)) for i in range(4)]
        sm_t = [(qk_t[i] * jnp.exp(log_d[i] - m_row[i])).astype(BF16) for i in range(4)]
        vext_t = [jnp.concatenate([vt[c, hh * A_DV:(hh + 1) * A_DV, :], ones_rows], axis=0)
                  for c, _, hh in chains]
        intra = [_dot(vext_t[i], sm_t[i]) for i in range(4)]
        inter = [_dot_nt(ct[i].astype(BF16), qh[i]) for i in range(4)]
        kw_t, m_new = [], []
        for i, (c, fwd, hh) in enumerate(chains):
            num_t = intra[i] + jnp.exp(li[i] - m_row[i]) * inter[i]
            den = num_t[A_DV:A_DV + 1, :]
            (yft if fwd else ybt)[c, hh * A_DV:(hh + 1) * A_DV, :] = num_t[:A_DV, :] * (
                1.0 / jnp.maximum(jnp.abs(den), jnp.exp(-m_row[i])))
            b_end = fcum[i][:, LC - 1:LC] if fwd else fcum[i][:, 0:1]
            m_new.append(jnp.maximum(b_end + m_prev[i], jnp.max(b_end + g[i], axis=1, keepdims=True)))
            kw_t.append((kst[c, hh * A_DK:(hh + 1) * A_DK, :].astype(F32)
                         * jnp.exp(b_end + g[i] - m_new[i])).astype(BF16))
            ct[i] = jnp.exp(b_end + m_prev[i] - m_new[i]) * ct[i]
        upd = [_dot_nt(vext_t[i], kw_t[i]) for i in range(4)]
        for i in range(4):
            ct_scr[i] = ct[i] + upd[i]
            m_scr[i] = jnp.broadcast_to(m_new[i], m_scr.shape[1:])
    _scan_schedule(n_chunks, step)

    for hh in range(2):
        sl = slice(hh * A_DV, (hh + 1) * A_DV)
        for c in range(n_chunks):
            y_t = yft[c, sl, :] + ybt[c, sl, :]
            yn = (y_t * lax.rsqrt(jnp.mean(y_t * y_t, axis=0, keepdims=True) + EPS)).T
            rs = slice(c * LC, (c + 1) * LC)
            y_ref[rs, sl] = (yn * ng_ref[:, sl] * _sigmoid(o_ref[rs, sl].astype(F32))).astype(y_ref.dtype)


def mlstm_scan(p, g_rows, conv_w, conv_b, norm_g, n_batch, n_chunks):
    rows = n_chunks * LC
    n_hp = A_HEADS // 2
    qk_w = 2 * A_DK
    v_w = 2 * A_DV
    kq, kv, ko = (A_HEADS * A_DK) // qk_w, (2 * A_HEADS * A_DK) // v_w, (2 * A_HEADS * A_DK + A_HEADS * A_DV) // v_w
    return pl.pallas_call(
        functools.partial(_mlstm_kernel, n_chunks=n_chunks),
        grid=(n_batch, n_hp),
        in_specs=[pl.BlockSpec((rows, qk_w), lambda b, h: (b, h)),
                  pl.BlockSpec((rows, qk_w), lambda b, h: (b, kq + h)),
                  pl.BlockSpec((rows, v_w), lambda b, h: (b, kv + h)),
                  pl.BlockSpec((rows, v_w), lambda b, h: (b, ko + h)),
                  pl.BlockSpec((1, 1, n_chunks, 8, LC), lambda b, h: (h, b, 0, 0, 0)),
                  pl.BlockSpec((3, qk_w), lambda b, h: (0, h)),
                  pl.BlockSpec((3, qk_w), lambda b, h: (0, kq + h)),
                  pl.BlockSpec((1, qk_w), lambda b, h: (0, h)),
                  pl.BlockSpec((1, qk_w), lambda b, h: (0, kq + h)),
                  pl.BlockSpec((1, v_w), lambda b, h: (0, h))],
        out_specs=pl.BlockSpec((rows, v_w), lambda b, h: (b, h)),
        out_shape=jax.ShapeDtypeStruct((n_batch * rows, A_HEADS * A_DV), BF16),
        scratch_shapes=[pltpu.VMEM((rows, qk_w), BF16), pltpu.VMEM((rows, qk_w), BF16),
                        pltpu.VMEM((n_chunks, qk_w, LC), BF16), pltpu.VMEM((n_chunks, v_w, LC), BF16),
                        pltpu.VMEM((n_chunks, v_w, LC), F32), pltpu.VMEM((n_chunks, v_w, LC), F32),
                        pltpu.VMEM((4, 2 * A_DV, A_DK), F32), pltpu.VMEM((4, 8, 128), F32)],
        compiler_params=_cparams("parallel", "parallel"),
        name="mlstm_scan",
    )(p, p, p, p, g_rows, conv_w, conv_w, conv_b, conv_b, norm_g)


GLA_LEVELS = int(math.log2(LC))


def _gla_level_masks():
    j = np.arange(LC)[:, None]
    s = np.arange(LC)[None, :]
    out = []
    for fwd in (True, False):
        for lev in range(GLA_LEVELS):
            m = 1 << lev
            same = (j // (2 * m)) == (s // (2 * m))
            hi_j, hi_s = (j % (2 * m)) >= m, (s % (2 * m)) >= m
            out.append(same & (hi_j & ~hi_s if fwd else ~hi_j & hi_s))
    return np.stack(out).astype(np.float32)


def _seg_ref(b, m, fwd):
    n, w = b.shape
    r = m - 1 if fwd else m
    if 2 * m >= 8:
        b3 = b.reshape(n // (2 * m), 2 * m, w)
        return jnp.broadcast_to(b3[:, r:r + 1, :], b3.shape).reshape(n, w)
    b3 = b.reshape(n // 8, 8, w)
    sub = lax.broadcasted_iota(I32, b3.shape, 1)
    out = None
    for blk in range(8 // (2 * m)):
        row = jnp.broadcast_to(b3[:, blk * 2 * m + r:blk * 2 * m + r + 1, :], b3.shape)
        out = row if out is None else jnp.where(sub >= blk * 2 * m, row, out)
    return out.reshape(n, w)


def _gla_kernel(q_ref, k_ref, v_ref, g_ref, low_ref, cwq_ref, cwk_ref, cwv_ref, cbq_ref, cbk_ref, cbv_ref,
                gw_ref, gb_ref, ng_ref, mask_ref, y_ref, qs, ks, vs, yf, yb, st_scr, *, n_chunks):
    n_rows = n_chunks * LC
    qs[...] = (_conv_silu(q_ref, cwq_ref, cbq_ref, n_rows) * (B_DK ** -0.5)).astype(BF16)
    ks[...] = _conv_silu(k_ref, cwk_ref, cbk_ref, n_rows).astype(BF16)
    vs[...] = _conv_silu(v_ref, cwv_ref, cbv_ref, n_rows).astype(BF16)
    st_scr[...] = jnp.zeros_like(st_scr)
    tril, triu = _tri_masks()
    tril_b, triu_b = tril.astype(BF16), triu.astype(BF16)
    eye = jnp.logical_and(tril, triu).astype(F32)

    def step(cf, cb):
        dirs = (0, 1)
        fwd = (True, False)
        rows = [pl.ds(pl.multiple_of(c * LC, LC), LC) for c in (cf, cb)]
        st = [st_scr[d] for d in dirs]
        pre = [jnp.dot(low_ref[rows[d], :], gw_ref[d], precision=lax.Precision.HIGHEST,
                       preferred_element_type=F32) + gb_ref[d:d + 1, :] for d in dirs]
        lg = [_log_sigmoid(pre[d]) * (1.0 / B_TAU) for d in dirs]
        b = [_dot_exact_lhs01(tril_b if fwd[d] else triu_b, lg[d]) for d in dirs]
        q = [qs[rows[d], :].astype(F32) for d in dirs]
        k = [ks[rows[d], :].astype(F32) for d in dirs]
        v = [vs[rows[d], :] for d in dirs]
        a = [eye * jnp.sum(q[d] * k[d], axis=-1, keepdims=True) for d in dirs]
        for lev in range(GLA_LEVELS):
            for d in dirs:
                decay = jnp.exp(-jnp.abs(b[d] - _seg_ref(b[d], 1 << lev, fwd[d])))
                qt = (q[d] * decay).astype(BF16)
                kt = (k[d] * decay).astype(BF16)
                a[d] = a[d] + mask_ref[d * GLA_LEVELS + lev] * _dot_nt(qt, kt)
        intra = [_dot(a[d].astype(BF16), v[d]) for d in dirs]
        inter = [_dot_nt((q[d] * jnp.exp(b[d])).astype(BF16), st[d].astype(BF16)) for d in dirs]
        yf[rows[0], :] = intra[0] + inter[0]
        yb[rows[1], :] = intra[1] + inter[1]
        b_end = [b[0][LC - 1:LC, :], b[1][0:1, :]]
        upd = [_dot_tn(v[d], (k[d] * jnp.exp(b_end[d] - b[d])).astype(BF16)) for d in dirs]
        for d in dirs:
            st_scr[d] = st[d] * jnp.exp(b_end[d]) + upd[d]
    _scan_schedule(n_chunks, step)

    y = yf[...] + yb[...]
    yn = y * lax.rsqrt(jnp.mean(y * y, axis=-1, keepdims=True) + EPS)
    y_ref[...] = (yn * ng_ref[...] * _silu(g_ref[...].astype(F32))).astype(y_ref.dtype)


def gla_scan(p, low, conv_w, conv_b, gate_w, gate_b, norm_g, n_batch, n_chunks):
    rows = n_chunks * LC
    kq, kv, kg = B_HEADS, (2 * B_HEADS * B_DK) // B_DV, (2 * B_HEADS * B_DK + B_HEADS * B_DV) // B_DV
    masks = jnp.asarray(_gla_level_masks(), F32)
    cmap = lambda b, h: (0, 0, 0)
    return pl.pallas_call(
        functools.partial(_gla_kernel, n_chunks=n_chunks),
        grid=(n_batch, B_HEADS),
        in_specs=[pl.BlockSpec((rows, B_DK), lambda b, h: (b, h)),
                  pl.BlockSpec((rows, B_DK), lambda b, h: (b, kq + h)),
                  pl.BlockSpec((rows, B_DV), lambda b, h: (b, kv + h)),
                  pl.BlockSpec((rows, B_DV), lambda b, h: (b, kg + h)),
                  pl.BlockSpec((rows, 128), lambda b, h: (b, 0)),
                  pl.BlockSpec((3, B_DK), lambda b, h: (0, h)),
                  pl.BlockSpec((3, B_DK), lambda b, h: (0, kq + h)),
                  pl.BlockSpec((3, B_DV), lambda b, h: (0, kv + h)),
                  pl.BlockSpec((1, B_DK), lambda b, h: (0, h)),
                  pl.BlockSpec((1, B_DK), lambda b, h: (0, kq + h)),
                  pl.BlockSpec((1, B_DV), lambda b, h: (0, kv + h)),
                  pl.BlockSpec((2, 128, B_DK), lambda b, h: (0, 0, h)),
                  pl.BlockSpec((2, B_DK), lambda b, h: (0, h)),
                  pl.BlockSpec((1, B_DV), lambda b, h: (0, h)),
                  pl.BlockSpec((2 * GLA_LEVELS, LC, LC), cmap)],
        out_specs=pl.BlockSpec((rows, B_DV), lambda b, h: (b, h)),
        out_shape=jax.ShapeDtypeStruct((n_batch * rows, B_HEADS * B_DV), BF16),
        scratch_shapes=[pltpu.VMEM((rows, B_DK), BF16), pltpu.VMEM((rows, B_DK), BF16),
                        pltpu.VMEM((rows, B_DV), BF16),
                        pltpu.VMEM((rows, B_DV), F32), pltpu.VMEM((rows, B_DV), F32),
                        pltpu.VMEM((2, B_DV, B_DK), F32)],
        compiler_params=_cparams("parallel", "parallel"),
        name="gla_scan",
    )(p, p, p, p, low, conv_w, conv_w, conv_w, conv_b, conv_b, conv_b, gate_w, gate_b, norm_g, masks)


FB = 256


def _dft_mats(n_len):
    n = 2 * n_len
    nfb = n_len // FB
    f = np.arange(n_len)[:, None]
    t = np.arange(n_len)[None, :]
    ang = 2.0 * np.pi * ((f * t) % n) / n
    alt = (-1.0) ** np.arange(n_len)
    cf, sf = np.cos(ang), -np.sin(ang)
    sf[0, :] = alt
    fwd = np.stack([cf.reshape(nfb, FB, n_len), sf.reshape(nfb, FB, n_len)], axis=1).reshape(n, n_len)
    ci, si = (2.0 / n) * np.cos(ang.T), -(2.0 / n) * np.sin(ang.T)
    ci[:, 0] = 1.0 / n
    si[:, 0] = alt / n
    inv = np.stack([ci.reshape(n_len, nfb, FB), si.reshape(n_len, nfb, FB)], axis=2).reshape(n_len, n)
    return fwd.astype(np.float32), inv.astype(np.float32)


def _hy_pre_kernel(x0_ref, x1_ref, v_ref, cw0, cw1, cw2, cb0, cb1, cb2, zc_ref, zl_ref, x0c_ref, x0l_ref,
                   *, n_rows):
    x0 = _conv3(x0_ref, cw0, cb0, n_rows)
    z = _conv3(x1_ref, cw1, cb1, n_rows) * _conv3(v_ref, cw2, cb2, n_rows)
    zc_ref[...] = z[:LC].astype(BF16)
    zl_ref[...] = z[LC:].astype(BF16)
    x0c_ref[...] = x0[:LC].astype(BF16)
    x0l_ref[...] = x0[LC:].astype(BF16)


def hyena_pre(p, conv_w, conv_b, n_batch, n_chunks):
    rows = n_chunks * LC
    d = p.shape[1] // 3
    w = 256
    nj = d // w
    s_len = rows - LC
    spec = lambda k: pl.BlockSpec((rows, w), lambda b, j: (b, k * nj + j))
    cws = lambda k: pl.BlockSpec((3, w), lambda b, j: (0, k * nj + j))
    cbs = lambda k: pl.BlockSpec((1, w), lambda b, j: (0, k * nj + j))
    oc = pl.BlockSpec((LC, w), lambda b, j: (b, j))
    ol = pl.BlockSpec((s_len, w), lambda b, j: (b, j))
    return pl.pallas_call(
        functools.partial(_hy_pre_kernel, n_rows=rows),
        grid=(n_batch, nj),
        in_specs=[spec(0), spec(1), spec(2), cws(0), cws(1), cws(2), cbs(0), cbs(1), cbs(2)],
        out_specs=[oc, ol, oc, ol],
        out_shape=[jax.ShapeDtypeStruct((n_batch * LC, d), BF16), jax.ShapeDtypeStruct((n_batch * s_len, d), BF16),
                   jax.ShapeDtypeStruct((n_batch * LC, d), BF16), jax.ShapeDtypeStruct((n_batch * s_len, d), BF16)],
        compiler_params=_cparams("parallel", "parallel"),
        name="hyena_pre",
    )(p, p, p, conv_w, conv_w, conv_w, conv_b, conv_b, conv_b)


def _filt_dft_kernel(f_ref, x_ref, o_ref):
    acc = _dot(f_ref[...], x_ref[0])
    half = acc.shape[1] // 2
    o_ref[0] = acc[:, :half] + acc[:, half:]


def filter_spectrum(fwd, hsum, hdiff):
    n_len, d = hsum.shape

    def hilo(a):
        hi = a.astype(BF16)
        return jnp.concatenate([hi, (a - hi.astype(F32)).astype(BF16)], axis=1)
    xs = jnp.stack([hilo(hsum), hilo(hdiff)])
    return pl.pallas_call(
        _filt_dft_kernel,
        grid=(2, n_len // FB),
        in_specs=[pl.BlockSpec((FB, n_len), lambda w, m: (2 * m + w, 0)),
                  pl.BlockSpec((1, n_len, 2 * d), lambda w, m: (w, 0, 0))],
        out_specs=pl.BlockSpec((1, FB, d), lambda w, m: (w, m, 0)),
        out_shape=jax.ShapeDtypeStruct((2, n_len, d), F32),
        compiler_params=_cparams("parallel", "parallel"),
        name="filter_spectrum",
    )(fwd, xs)


def _dft_fwd_kernel(f_ref, z_ref, kr_ref, ki_ref, y_ref):
    acc = _dot(f_ref[...], z_ref[...])
    zr, zi = acc[:FB], acc[FB:]
    kr, ki = kr_ref[...], ki_ref[...]
    packed = jnp.logical_and(pl.program_id(1) == 0, lax.broadcasted_iota(I32, (FB, 1), 0) == 0)
    y_ref[:FB] = (zr * kr - jnp.where(packed, 0.0, zi * ki)).astype(BF16)
    y_ref[FB:] = jnp.where(packed, zi * ki, zr * ki + zi * kr).astype(BF16)


def dft_forward(fwd, z, kr, ki, n_batch):
    n_len, d = kr.shape
    nfb = n_len // FB
    return pl.pallas_call(
        _dft_fwd_kernel,
        grid=(n_batch, nfb),
        in_specs=[pl.BlockSpec((2 * FB, n_len), lambda b, m: (m, 0)),
                  pl.BlockSpec((n_len, d), lambda b, m: (b, 0)),
                  pl.BlockSpec((FB, d), lambda b, m: (m, 0)),
                  pl.BlockSpec((FB, d), lambda b, m: (m, 0))],
        out_specs=pl.BlockSpec((2 * FB, d), lambda b, m: (b * nfb + m, 0)),
        out_shape=jax.ShapeDtypeStruct((n_batch * 2 * n_len, d), BF16),
        compiler_params=_cparams("parallel", "arbitrary"),
        name="dft_forward",
    )(fwd, z, kr, ki)


def _dft_inv_kernel(ic_ref, yc_ref, zc_ref, x0c_ref, il_ref, yl_ref, zl_ref, x0l_ref, skip_ref, o_ref):
    def emit(i_ref, y_ref, z_ref, x0_ref):
        conv = _dot(i_ref[...], y_ref[...])
        o_ref[...] = (x0_ref[...].astype(F32) * (conv + z_ref[...].astype(F32) * skip_ref[...])).astype(o_ref.dtype)

    @pl.when(pl.program_id(1) == 0)
    def _():
        emit(ic_ref, yc_ref, zc_ref, x0c_ref)

    @pl.when(pl.program_id(1) > 0)
    def _():
        emit(il_ref, yl_ref, zl_ref, x0l_ref)


def dft_inverse(ctx_parts, lat_parts, skip, n_batch, tiles_per_batch):
    inv_c, yf_c, z_c, x0_c = ctx_parts
    inv_l, yf_l, z_l, x0_l = lat_parts
    d = z_l.shape[1]
    n_l = z_l.shape[0] // n_batch
    nt = n_l // TM
    lt = lambda m: jnp.maximum(m - 1, 0)
    return pl.pallas_call(
        _dft_inv_kernel,
        grid=(n_batch, tiles_per_batch),
        in_specs=[pl.BlockSpec((TM, 2 * LC), lambda b, m: (0, 0)),
                  pl.BlockSpec((2 * LC, d), lambda b, m: (b, 0)),
                  pl.BlockSpec((TM, d), lambda b, m: (b, 0)),
                  pl.BlockSpec((TM, d), lambda b, m: (b, 0)),
                  pl.BlockSpec((TM, 2 * n_l), lambda b, m: (lt(m), 0)),
                  pl.BlockSpec((2 * n_l, d), lambda b, m: (b, 0)),
                  pl.BlockSpec((TM, d), lambda b, m: (b * nt + lt(m), 0)),
                  pl.BlockSpec((TM, d), lambda b, m: (b * nt + lt(m), 0)),
                  pl.BlockSpec((1, d), lambda b, m: (0, 0))],
        out_specs=pl.BlockSpec((TM, d), lambda b, m: (b * tiles_per_batch + m, 0)),
        out_shape=jax.ShapeDtypeStruct((n_batch * tiles_per_batch * TM, d), BF16),
        compiler_params=_cparams("parallel", "arbitrary"),
        name="dft_inverse",
    )(inv_c, yf_c, z_c, x0_c, inv_l, yf_l, z_l, x0_l, skip.reshape(1, d))


def _hyena_filters(n_len, w1, b1, w_mid, b_mid, w_out, freq):
    hp = lax.Precision.HIGHEST
    d = w_out.shape[1] // 2
    n_bands = (w1.shape[0] - 1) // 2
    t = jnp.linspace(0.0, 1.0, n_len, dtype=F32)[:, None]
    pos = jnp.arange(n_len, dtype=F32)[:, None]
    bands = jnp.linspace(1e-4, n_bands - 1, n_bands, dtype=F32)[None, :]
    ang = (2.0 * math.pi / n_len) * pos * bands
    feats = jnp.concatenate([t, jnp.cos(ang), -jnp.sin(ang)], -1)
    h = jnp.sin(freq * (jnp.dot(feats, w1, precision=hp) + b1))
    for m in range(w_mid.shape[0]):
        h = jnp.sin(freq * (jnp.dot(h, w_mid[m], precision=hp) + b_mid[m]))
    h = jnp.dot(h, w_out, precision=hp).reshape(n_len, 2, d)
    deltas = jnp.abs(jnp.linspace(math.log(1e-2) / 1.5, math.log(1e-2) / 0.3, d, dtype=F32))
    h = h * jnp.exp(-t * deltas)[:, None, :]
    hf, hb = h[:, 0], h[:, 1]
    l1 = jnp.sum(jnp.abs(hf), 0) + jnp.sum(jnp.abs(hb[1:]), 0)
    return hf / l1, hb / l1


def hyena_mix(p, conv_w, conv_b, filt, skip, n_batch, n_chunks):
    rows = n_chunks * LC
    zc, zl, x0c, x0l = hyena_pre(p, conv_w, conv_b, n_batch, n_chunks)
    parts = []
    for z, x0 in ((zc, x0c), (zl, x0l)):
        n_len = z.shape[0] // n_batch
        fwd_np, inv_np = _dft_mats(n_len)
        fwd, inv = jnp.asarray(fwd_np, F32).astype(BF16), jnp.asarray(inv_np, F32).astype(BF16)
        hf, hb = _hyena_filters(n_len, *filt)
        hb0 = hb.at[0].set(0.0)
        hsum, hdiff = hf + hb0, hf - hb0
        spec = filter_spectrum(fwd, hsum, hdiff)
        alt = jnp.asarray((-1.0) ** np.arange(n_len), F32)[:, None]
        kr, ki = spec[0], spec[1].at[0].set(jnp.sum(alt * hsum, axis=0))
        parts.append((inv, dft_forward(fwd, z, kr, ki, n_batch), z, x0))
    return dft_inverse(parts[0], parts[1], skip, n_batch, n_chunks)


P_DTYPE = BF16
N_MAIN = 3072
N_EXTRA = 128


def _pad_cols(w, n):
    return jnp.pad(w, ((0, 0), (0, n - w.shape[1])))


def _scan_order_kernel(x_ref, o_ref, *, ctx_len, rows_per_col):
    o_ref[0:ctx_len, :] = x_ref[0:ctx_len, :]
    for w in range(GRID_W):
        o_ref[ctx_len + w * rows_per_col:ctx_len + (w + 1) * rows_per_col, :] = (
            x_ref[pl.ds(ctx_len + w, rows_per_col, stride=GRID_W), :])


def to_scan_order_f32(a, n_batch, ctx_len):
    t, d = a.shape
    rows = t // n_batch
    cw = LANES
    return pl.pallas_call(
        functools.partial(_scan_order_kernel, ctx_len=ctx_len, rows_per_col=(rows - ctx_len) // GRID_W),
        grid=(n_batch, d // cw),
        in_specs=[pl.BlockSpec((rows, cw), lambda b, j: (b, j))],
        out_specs=pl.BlockSpec((rows, cw), lambda b, j: (b, j)),
        out_shape=jax.ShapeDtypeStruct(a.shape, a.dtype),
        compiler_params=_cparams("parallel", "parallel"),
        name="to_scan_order",
    )(a)


def _from_scan_order(a, n_batch, ctx_len):
    w = a.shape[1]
    a3 = a.reshape(n_batch, -1, w)
    lat = a3[:, ctx_len:]
    s_len = lat.shape[1]
    lat = lat.reshape(n_batch, GRID_W, s_len // GRID_W, w).transpose(0, 2, 1, 3).reshape(n_batch, s_len, w)
    return jnp.concatenate([a3[:, :ctx_len], lat], axis=1).reshape(-1, w)


def kernel(x, c, ctx, c_ctx, ada_w, ada_b, norm1_g, norm2_g, final_g, ml_w_in, ml_b_gate, ml_conv_w, ml_conv_b, ml_norm_g, ml_w_out, gla_w_in, gla_conv_w, gla_conv_b, gla_gate_w2, gla_gate_b, gla_norm_g, gla_w_out, hy_w_in, hy_b_in, hy_conv_w, hy_conv_b, hy_filt_w1, hy_filt_b1, hy_filt_w_mid, hy_filt_b_mid, hy_filt_w_out, hy_filt_freq, hy_skip, hy_w_out, hy_b_out, router_w, router_b, exp_w1, exp_w3, exp_w2):
    n_batch, s_len, d = x.shape
    ctx_len = ctx.shape[1]
    depth = ada_w.shape[0]
    assert ctx_len == LC == TM and s_len % LC == 0 and s_len % GRID_W == 0
    n_chunks = (ctx_len + s_len) // LC
    t_all = n_batch * n_chunks * LC
    lat_tiles = s_len // TM

    mod_rows = jnp.zeros((16, d), F32).at[0].set(c_ctx).at[1:1 + n_batch].set(c)
    mods = ada_mods(mod_rows, ada_w, ada_b).reshape(depth, 16, 6, d)
    xs = jnp.concatenate([ctx, x], axis=1).reshape(t_all, d)
    router_wt = router_w.T
    zeros_d = jnp.zeros((d,), F32)
    out = None

    for i in range(depth):
        last = i == depth - 1
        kind, j = i % 3, i // 3
        if kind == 0:
            w_in = _pad_cols(ml_w_in[j], N_MAIN + N_EXTRA).astype(BF16)
            bias = jnp.zeros((N_MAIN + N_EXTRA,), F32).at[N_MAIN:N_MAIN + 4 * A_HEADS].set(ml_b_gate[j])
            p, pg = in_proj(xs, mods[i], norm1_g[i], w_in, bias, N_MAIN, n_chunks, P_DTYPE)
            g4 = pg[:, :4 * A_HEADS].reshape(t_all, 4, A_HEADS // 2, 2)
            g_cols = g4.transpose(2, 0, 1, 3).reshape(A_HEADS // 2, t_all, 8)
            g_rows = g_cols.reshape(A_HEADS // 2, n_batch, n_chunks, LC, 8).transpose(0, 1, 2, 4, 3)
            y = mlstm_scan(p, g_rows, ml_conv_w[j], ml_conv_b[j].reshape(1, -1),
                           ml_norm_g[j].reshape(1, -1), n_batch, n_chunks)
            w_out, b_out = ml_w_out[j], zeros_d
        elif kind == 1:
            w_in = _pad_cols(gla_w_in[j], N_MAIN + N_EXTRA).astype(BF16)
            bias = jnp.zeros((N_MAIN + N_EXTRA,), F32)
            p, pg = in_proj(to_scan_order_f32(xs, n_batch, ctx_len), mods[i], norm1_g[i], w_in, bias, N_MAIN,
                            n_chunks, P_DTYPE)
            gate_w = jnp.zeros((2, 128, B_HEADS * B_DK), F32)
            gate_w = gate_w.at[0, :B_RANK].set(gla_gate_w2[j, 0]).at[1, B_RANK:2 * B_RANK].set(gla_gate_w2[j, 1])
            y = gla_scan(p, pg, gla_conv_w[j], gla_conv_b[j].reshape(1, -1), gate_w, gla_gate_b[j],
                         gla_norm_g[j].reshape(1, -1), n_batch, n_chunks)
            y = _from_scan_order(y, n_batch, ctx_len)
            w_out, b_out = gla_w_out[j], zeros_d
        else:
            (p,) = in_proj(xs, mods[i], norm1_g[i], hy_w_in[j].astype(BF16), hy_b_in[j], N_MAIN, n_chunks, P_DTYPE)
            filt = (hy_filt_w1[j], hy_filt_b1[j], hy_filt_w_mid[j], hy_filt_b_mid[j], hy_filt_w_out[j],
                    hy_filt_freq[j])
            y = hyena_mix(p, hy_conv_w[j], hy_conv_b[j].reshape(1, -1), filt, hy_skip[j], n_batch, n_chunks)
            w_out, b_out = hy_w_out[j], hy_b_out[j]

        x2, h2, ids, gates, ranks, counts = out_route(
            xs, y, mods[i], w_out.astype(BF16), b_out, norm2_g[i], router_wt, router_b, n_chunks, 1, last)
        n_tok = x2.shape[0]
        dest_tiles, n_slots, block_exp, n_used, pad_end = moe_plan(ids, ranks, counts, n_tok)
        xb = moe_dispatch(h2, dest_tiles, pad_end, n_used, n_slots)
        yb = moe_experts(xb, block_exp, n_used, exp_w1, exp_w3, exp_w2, i)
        if last:
            out = moe_combine(x2, yb, dest_tiles, gates.T, mods[i], lambda t: 1 + t // lat_tiles, final_g)
        else:
            xs = moe_combine(x2, yb, dest_tiles, gates.T, mods[i],
                             lambda t: jnp.where(t % n_chunks == 0, 0, 1 + t // n_chunks))
    return out.reshape(n_batch, s_len, d)
```
